```python
import math
import jax
import jax.numpy as jnp
from jax import lax
import numpy as np

D_MODEL = 1024
BATCH = 8
SEQ = 2048
DEPTH = 2

CHUNK = 64
Q_BLOCK = 128
N_MEM = 256
A_HEADS = 8
A_HEAD_DIM = 64
A_VAL_DIM = 2 * A_HEAD_DIM
A_QK_WIDTH = A_HEADS * 2 * A_HEAD_DIM
A_V_WIDTH = A_HEADS * A_VAL_DIM
ROPE_THETA = 10000.0
POOL_WINDOWS = (2, 4, 8, 16)
POOL_GROUPS = 4
POOL_WIDTH = D_MODEL
POOL_GROUP_DIM = POOL_WIDTH // POOL_GROUPS
R_HEAD_DIM = 64
R_WIDTH = D_MODEL
R_HEADS = R_WIDTH // R_HEAD_DIM
DECAY_LORA = max(32, int(round(1.8 * D_MODEL ** 0.5 / 32)) * 32)
AAA_LORA = max(32, int(round(1.8 * D_MODEL ** 0.5 / 32)) * 32)
GATE_LORA = max(32, int(round(0.6 * D_MODEL ** 0.8 / 32)) * 32)
LORA_WIDTH = DECAY_LORA + AAA_LORA + GATE_LORA
RWKV_IN_WIDTH = 3 * R_WIDTH + LORA_WIDTH
N_BRANCH = 3
P_IN = 2 * A_QK_WIDTH + A_V_WIDTH + POOL_WIDTH + RWKV_IN_WIDTH + N_BRANCH * D_MODEL
X_HEADS = 4
X_HEAD_DIM = D_MODEL // X_HEADS
D_FF = 4 * D_MODEL
DN_ALPHA = (2 * DEPTH) ** 0.25
DN_BETA = (8 * DEPTH) ** -0.25
LN_EPS = 1e-5
RMS_EPS = 1e-5
RWKV_GN_EPS = 64e-5
NEG_INF = -1e30
F32 = jnp.float32

kernel_name = 'hybrid_diffattn_pool_rwkv7_deepnorm'


def _split_points(sizes):
    pts, acc = [], 0
    for s in sizes[:-1]:
        acc += s
        pts.append(acc)
    return pts


def layer_norm(x, g, b, eps=LN_EPS):
    xf = x.astype(F32)
    mu = jnp.mean(xf, -1, keepdims=True)
    var = jnp.mean(jnp.square(xf - mu), -1, keepdims=True)
    return ((xf - mu) * lax.rsqrt(var + eps) * g.astype(F32) + b.astype(F32)).astype(x.dtype)


def rope_tables(positions, dim):
    inv_freq = 1.0 / (ROPE_THETA ** (jnp.arange(0, dim, 2, dtype=F32) / dim))
    ang = positions.astype(F32)[..., None] * inv_freq
    return jnp.cos(ang), jnp.sin(ang)


def apply_rope(t, cos, sin):
    c = cos[:, :, None, None, :]
    s = sin[:, :, None, None, :]
    t1, t2 = jnp.split(t.astype(F32), 2, axis=-1)
    return jnp.concatenate([t1 * c - t2 * s, t2 * c + t1 * s], axis=-1)


def diff_attention(q, k, v, lam, subln_g, lam_init):
    bsz, seq = q.shape[0], q.shape[1]
    scale = A_HEAD_DIM ** -0.5
    vf = v.astype(F32)
    chunk_id = jnp.arange(seq) // CHUNK
    outs = []
    for i in range(seq // Q_BLOCK):
        lo, hi = i * Q_BLOCK, (i + 1) * Q_BLOCK
        s = jnp.einsum('bqhmd,bkhmd->bhmqk', q[:, lo:hi], k[:, :hi]) * scale
        allowed = chunk_id[None, :hi] <= chunk_id[lo:hi, None]
        s = jnp.where(allowed, s, NEG_INF)
        p = jax.nn.softmax(s, axis=-1)
        a = p[:, :, 0] - lam * p[:, :, 1]
        outs.append(jnp.einsum('bhqk,bkhd->bqhd', a, vf[:, :hi]))
    o = jnp.concatenate(outs, axis=1)
    o = o * lax.rsqrt(jnp.mean(jnp.square(o), -1, keepdims=True) + RMS_EPS) * subln_g.astype(F32)
    return (o * (1.0 - lam_init)).reshape(bsz, seq, A_V_WIDTH).astype(v.dtype)


def pool_mixer(p, w_grp, scale):
    bsz, seq, _ = p.shape
    pf = p.astype(F32).reshape(bsz, seq, POOL_GROUPS, POOL_GROUP_DIM)
    cs = jnp.concatenate([jnp.zeros((bsz, 1, POOL_GROUPS, POOL_GROUP_DIM), F32), jnp.cumsum(pf, axis=1)], axis=1)
    t = jnp.arange(seq)
    outs = []
    for g, w in enumerate(POOL_WINDOWS):
        cs_g = cs[:, :, g]
        window_sum = cs_g[:, 1:] - cs_g[:, jnp.maximum(t + 1 - w, 0)]
        count = jnp.minimum(t + 1, w).astype(F32)[None, :, None]
        outs.append(window_sum / count - pf[:, :, g])
    d = jnp.stack(outs, axis=2)
    y = jnp.einsum('bsgc,gce->bsge', d, w_grp.astype(F32)).reshape(bsz, seq, POOL_WIDTH)
    return (y * scale.astype(F32)).astype(p.dtype)


def token_shift(u, mu):
    prev = jnp.pad(u, ((0, 0), (1, 0), (0, 0)))[:, :-1]
    return u + (prev - u) * mu


def rwkv7_mixer(r, k, v, xw, xa, xg, w0, w2, a0, a2, g2, k_k, k_a, r_k, lnx_g, lnx_b):
    bsz, seq, width = r.shape
    out_dtype = r.dtype
    r, k, v, xw, xa, xg = [t.astype(F32) for t in (r, k, v, xw, xa, xg)]
    log_w = -jax.nn.softplus(-(w0.astype(F32) + jnp.tanh(xw) @ w2.astype(F32))) - 0.5
    decay = jnp.exp(-jnp.exp(log_w))
    a = jax.nn.sigmoid(a0.astype(F32) + xa @ a2.astype(F32))
    g = jax.nn.sigmoid(xg) @ g2.astype(F32)
    heads = lambda t: t.reshape(bsz, seq, R_HEADS, R_HEAD_DIM)
    kk = heads(k * k_k.astype(F32))
    kk = kk / jnp.maximum(jnp.sqrt(jnp.sum(jnp.square(kk), -1, keepdims=True)), 1e-12)
    k = k * (1.0 + (a - 1.0) * k_a.astype(F32))
    rh, kh, vh, ah = heads(r), heads(k), heads(v), heads(a)

    def step(state, inp):
        r_t, w_t, k_t, v_t, a_t, b_t = inp
        state = (state * w_t[:, :, None, :]
                 + jnp.einsum('bhij,bhj->bhi', state, a_t)[..., None] * b_t[:, :, None, :]
                 + v_t[..., None] * k_t[:, :, None, :])
        return state, jnp.einsum('bhij,bhj->bhi', state, r_t)

    tm = lambda t: jnp.swapaxes(t, 0, 1)
    s0 = jnp.zeros((bsz, R_HEADS, R_HEAD_DIM, R_HEAD_DIM), F32)
    _, y = lax.scan(step, s0, (tm(rh), tm(heads(decay)), tm(kh), tm(vh), tm(-kk), tm(kk * ah)))
    y = tm(y)
    mu = jnp.mean(y, -1, keepdims=True)
    var = jnp.mean(jnp.square(y - mu), -1, keepdims=True)
    y = ((y - mu) * lax.rsqrt(var + RWKV_GN_EPS)).reshape(bsz, seq, width)
    y = y * lnx_g.astype(F32) + lnx_b.astype(F32)
    bonus = jnp.sum(rh * kh * r_k.astype(F32), -1, keepdims=True) * vh
    y = (y + bonus.reshape(bsz, seq, width)) * g
    return y.astype(out_dtype)


def cross_attention(h, mem, w_q, w_kv, w_o):
    bsz, seq, _ = h.shape
    q = (h @ w_q).reshape(bsz, seq, X_HEADS, X_HEAD_DIM).astype(F32)
    k, v = jnp.split(mem @ w_kv, 2, axis=-1)
    k = k.reshape(bsz, -1, X_HEADS, X_HEAD_DIM).astype(F32)
    v = v.reshape(bsz, -1, X_HEADS, X_HEAD_DIM).astype(F32)
    p = jax.nn.softmax(jnp.einsum('bqhd,bkhd->bhqk', q, k) * X_HEAD_DIM ** -0.5, axis=-1)
    o = jnp.einsum('bhqk,bkhd->bqhd', p, v).reshape(bsz, seq, D_MODEL).astype(h.dtype)
    return o @ w_o


def setup_inputs(seed: int = 0) -> dict:
    key = jax.random.key(seed)
    ks = iter(jax.random.split(key, 64))
    L = DEPTH

    def nrm(shape, scale):
        return jax.random.normal(next(ks), shape, F32) * scale

    def gain(shape):
        return 1.0 + nrm(shape, 0.02)

    x = nrm((BATCH, SEQ, D_MODEL), 1.0)
    mem = nrm((BATCH, N_MEM, D_MODEL), 1.0)
    start = jax.random.randint(next(ks), (BATCH, 1), 0, 4096, dtype=jnp.int32)
    positions = start + jnp.arange(SEQ, dtype=jnp.int32)[None, :]
    ramp = (jnp.arange(R_WIDTH, dtype=F32) / (R_WIDTH - 1)) ** 0.9
    return {
        'x': x, 'mem': mem, 'positions': positions,
        'ln_in_g': gain((D_MODEL,)), 'ln_in_b': nrm((D_MODEL,), 0.02),
        'w_in': nrm((L, D_MODEL, P_IN), D_MODEL ** -0.5),
        'b_gate': nrm((L, N_BRANCH, D_MODEL), 0.02),
        'lam_q1': nrm((L, A_HEAD_DIM), 0.1), 'lam_k1': nrm((L, A_HEAD_DIM), 0.1),
        'lam_q2': nrm((L, A_HEAD_DIM), 0.1), 'lam_k2': nrm((L, A_HEAD_DIM), 0.1),
        'attn_subln_g': gain((L, A_VAL_DIM)),
        'w_br_attn': nrm((L, A_V_WIDTH, D_MODEL), A_V_WIDTH ** -0.5),
        'pool_w': nrm((L, POOL_GROUPS, POOL_GROUP_DIM, POOL_GROUP_DIM), POOL_GROUP_DIM ** -0.5),
        'pool_scale': gain((L, POOL_WIDTH)),
        'w_br_pool': nrm((L, POOL_WIDTH, D_MODEL), POOL_WIDTH ** -0.5),
        'rwkv_mu': jax.random.uniform(next(ks), (L, RWKV_IN_WIDTH), F32),
        'rwkv_w0': -6.0 + 5.0 * ramp[None, :] + nrm((L, R_WIDTH), 0.1),
        'rwkv_w2': nrm((L, DECAY_LORA, R_WIDTH), 0.1 * DECAY_LORA ** -0.5),
        'rwkv_a0': nrm((L, R_WIDTH), 0.1),
        'rwkv_a2': nrm((L, AAA_LORA, R_WIDTH), 0.1 * AAA_LORA ** -0.5),
        'rwkv_g2': nrm((L, GATE_LORA, R_WIDTH), GATE_LORA ** -0.5),
        'rwkv_k_k': 0.85 + nrm((L, R_WIDTH), 0.02),
        'rwkv_k_a': gain((L, R_WIDTH)),
        'rwkv_r_k': nrm((L, R_HEADS, R_HEAD_DIM), 0.1),
        'rwkv_lnx_g': gain((L, R_WIDTH)), 'rwkv_lnx_b': nrm((L, R_WIDTH), 0.02),
        'w_br_rwkv': nrm((L, R_WIDTH, D_MODEL), R_WIDTH ** -0.5),
        'w_out': nrm((L, D_MODEL, D_MODEL), DN_BETA * D_MODEL ** -0.5),
        'ln1_g': gain((L, D_MODEL)), 'ln1_b': nrm((L, D_MODEL), 0.02),
        'w_xq': nrm((L, D_MODEL, D_MODEL), D_MODEL ** -0.5),
        'w_xkv': nrm((L, D_MODEL, 2 * D_MODEL), D_MODEL ** -0.5),
        'w_xo': nrm((L, D_MODEL, D_MODEL), DN_BETA * D_MODEL ** -0.5),
        'ln2_g': gain((L, D_MODEL)), 'ln2_b': nrm((L, D_MODEL), 0.02),
        'w_ff1': nrm((L, D_MODEL, D_FF), D_MODEL ** -0.5),
        'w_ff2': nrm((L, D_FF, D_MODEL), DN_BETA * D_FF ** -0.5),
        'ln3_g': gain((L, D_MODEL)), 'ln3_b': nrm((L, D_MODEL), 0.02),
    }


def reference(x, mem, positions, ln_in_g, ln_in_b, w_in, b_gate,
              lam_q1, lam_k1, lam_q2, lam_k2, attn_subln_g, w_br_attn,
              pool_w, pool_scale, w_br_pool,
              rwkv_mu, rwkv_w0, rwkv_w2, rwkv_a0, rwkv_a2, rwkv_g2, rwkv_k_k, rwkv_k_a, rwkv_r_k,
              rwkv_lnx_g, rwkv_lnx_b, w_br_rwkv,
              w_out, ln1_g, ln1_b, w_xq, w_xkv, w_xo, ln2_g, ln2_b,
              w_ff1, w_ff2, ln3_g, ln3_b):
    bsz, seq, _ = x.shape
    in_pts = _split_points([A_QK_WIDTH, A_QK_WIDTH, A_V_WIDTH, POOL_WIDTH, RWKV_IN_WIDTH, N_BRANCH * D_MODEL])
    rwkv_pts = _split_points([R_WIDTH, R_WIDTH, R_WIDTH, DECAY_LORA, AAA_LORA, GATE_LORA])
    cos, sin = rope_tables(positions, A_HEAD_DIM)
    h = layer_norm(x, ln_in_g, ln_in_b)
    for l in range(DEPTH):
        u = h @ w_in[l]
        u_q, u_k, u_v, u_pool, u_rwkv, u_gate = jnp.split(u, in_pts, axis=-1)
        q = apply_rope(u_q.reshape(bsz, seq, A_HEADS, 2, A_HEAD_DIM), cos, sin)
        k = apply_rope(u_k.reshape(bsz, seq, A_HEADS, 2, A_HEAD_DIM), cos, sin)
        v = u_v.reshape(bsz, seq, A_HEADS, A_VAL_DIM)
        lam_init = 0.8 - 0.6 * math.exp(-0.3 * l)
        lam = (jnp.exp(jnp.sum(lam_q1[l].astype(F32) * lam_k1[l].astype(F32)))
               - jnp.exp(jnp.sum(lam_q2[l].astype(F32) * lam_k2[l].astype(F32))) + lam_init)
        y_a = diff_attention(q, k, v, lam, attn_subln_g[l], lam_init) @ w_br_attn[l]
        y_b = pool_mixer(u_pool, pool_w[l], pool_scale[l]) @ w_br_pool[l]
        r_r, r_k, r_v, r_xw, r_xa, r_xg = jnp.split(token_shift(u_rwkv, rwkv_mu[l]), rwkv_pts, axis=-1)
        y_c = rwkv7_mixer(r_r, r_k, r_v, r_xw, r_xa, r_xg, rwkv_w0[l], rwkv_w2[l], rwkv_a0[l], rwkv_a2[l],
                          rwkv_g2[l], rwkv_k_k[l], rwkv_k_a[l], rwkv_r_k[l], rwkv_lnx_g[l], rwkv_lnx_b[l]) @ w_br_rwkv[l]
        gates = jax.nn.sigmoid((u_gate.reshape(bsz, seq, N_BRANCH, D_MODEL) + b_gate[l]).astype(F32))
        merged = gates[:, :, 0] * y_a + gates[:, :, 1] * y_b + gates[:, :, 2] * y_c
        h = layer_norm(DN_ALPHA * h + merged.astype(h.dtype) @ w_out[l], ln1_g[l], ln1_b[l])
        h = layer_norm(DN_ALPHA * h + cross_attention(h, mem, w_xq[l], w_xkv[l], w_xo[l]), ln2_g[l], ln2_b[l])
        ff = jnp.square(jax.nn.relu(h @ w_ff1[l])) @ w_ff2[l]
        h = layer_norm(DN_ALPHA * h + ff, ln3_g[l], ln3_b[l])
    return h
```

```python
import functools
import math

import jax
import jax.numpy as jnp
from jax import lax
from jax.experimental import pallas as pl
from jax.experimental.pallas import tpu as pltpu

F32 = jnp.float32
BF16 = jnp.bfloat16

D_MODEL = 1024
DEPTH = 2
CHUNK = 64
A_HEADS = 8
A_HEAD_DIM = 64
A_VAL_DIM = 2 * A_HEAD_DIM
ROPE_THETA = 10000.0
POOL_WINDOWS = (2, 4, 8, 16)
POOL_GROUP_DIM = D_MODEL // len(POOL_WINDOWS)
R_HEAD_DIM = 64
DECAY_LORA = 64
AAA_LORA = 64
GATE_LORA = 160
X_HEADS = 4
X_HEAD_DIM = D_MODEL // X_HEADS
D_FF = 4 * D_MODEL
DN_ALPHA = (2 * DEPTH) ** 0.25
LN_EPS = 1e-5
RMS_EPS = 1e-5
RWKV_GN_EPS = 64e-5
NEG_INF = -1e30

V7X_LANES = 128
V7X_SUBLANES = 8
V7X_MXU_DIM = 256
V7X_VMEM_LIMIT_BYTES = 56 * 1024 * 1024

ROW_TILE = 1024
COL_TILE = 512
FUSED_ROW_TILE = 256
MLP_ROW_TILE = 512
MLP_FF_TILE = 1024
ATTN_TILE = 128
RW_CHUNK = 64
RW_QUAD = V7X_MXU_DIM
RW_PREP_TILE = 128
RW_SCAN_TILE = 256
LORA_PAD = 512
HIGHEST = lax.Precision.HIGHEST


def _tile(n, pref):
    return pref if n % pref == 0 else n


def _params(*sem):
    return pltpu.CompilerParams(dimension_semantics=sem, vmem_limit_bytes=V7X_VMEM_LIMIT_BYTES)


def _ln(z, g, b, eps):
    mu = jnp.mean(z, -1, keepdims=True)
    d = z - mu
    var = jnp.mean(d * d, -1, keepdims=True)
    return d * lax.rsqrt(var + eps) * g + b


def _dot(a, b, precision=None):
    return jnp.dot(a, b, preferred_element_type=F32, precision=precision)


def _dot_nt(a, b, precision=None):
    return lax.dot_general(a, b, (((1,), (1,)), ((), ())), preferred_element_type=F32, precision=precision)


def _dot_tn(a, b, precision=None):
    return lax.dot_general(a, b, (((0,), (0,)), ((), ())), preferred_element_type=F32, precision=precision)


def _ln_in_kernel(x_ref, g_ref, b_ref, h32_ref, h16_ref):
    h = _ln(x_ref[...], g_ref[...], b_ref[...], LN_EPS)
    h32_ref[...] = h
    h16_ref[...] = h.astype(BF16)


def _ln_in(x2d, g, b):
    t, d = x2d.shape
    tm = _tile(t, ROW_TILE)
    row = pl.BlockSpec((tm, d), lambda i: (i, 0))
    vec = pl.BlockSpec((1, d), lambda i: (0, 0))
    return pl.pallas_call(
        _ln_in_kernel,
        grid=(t // tm,),
        in_specs=[row, vec, vec],
        out_specs=[row, row],
        out_shape=[jax.ShapeDtypeStruct((t, d), F32), jax.ShapeDtypeStruct((t, d), BF16)],
        compiler_params=_params("parallel"),
        name="ln_in",
    )(x2d, g.reshape(1, d), b.reshape(1, d))


def _proj_kernel(x_ref, w_ref, o_ref):
    o_ref[...] = _dot(x_ref[...].astype(BF16), w_ref[...]).astype(o_ref.dtype)


def _proj_rope_kernel(x_ref, w_ref, cos_ref, sin_ref, o_ref):
    t = _dot(x_ref[...], w_ref[...])
    cos = cos_ref[...]
    sin = sin_ref[...]
    lane = lax.broadcasted_iota(jnp.int32, cos.shape, 1)
    first_half = (lane & (A_HEAD_DIM - 1)) < A_HEAD_DIM // 2
    for j in range(t.shape[1] // V7X_LANES):
        blk = t[:, j * V7X_LANES:(j + 1) * V7X_LANES]
        partner = jnp.where(first_half,
                            pltpu.roll(blk, V7X_LANES - A_HEAD_DIM // 2, 1),
                            pltpu.roll(blk, A_HEAD_DIM // 2, 1))
        o_ref[:, j * V7X_LANES:(j + 1) * V7X_LANES] = (blk * cos + partner * sin).astype(o_ref.dtype)


def _proj(x, w16, out_dtype, rope=None, name="proj"):
    m, k = x.shape
    n = w16.shape[1]
    tm = _tile(m, ROW_TILE)
    tn = _tile(n, COL_TILE)
    in_specs = [pl.BlockSpec((tm, k), lambda i, j: (i, 0)), pl.BlockSpec((k, tn), lambda i, j: (0, j))]
    args = [x, w16]
    body = _proj_kernel
    if rope is not None:
        in_specs += [pl.BlockSpec((tm, V7X_LANES), lambda i, j: (i, 0))] * 2
        args += list(rope)
        body = _proj_rope_kernel
    return pl.pallas_call(
        body,
        grid=(m // tm, n // tn),
        in_specs=in_specs,
        out_specs=pl.BlockSpec((tm, tn), lambda i, j: (i, j)),
        out_shape=jax.ShapeDtypeStruct((m, n), out_dtype),
        compiler_params=_params("parallel", "parallel"),
        name=name,
    )(*args)


def _diff_attn_kernel(lam_ref, g_ref, q_ref, k_ref, v_ref, o_ref, *, lam_init):
    i = pl.program_id(2)
    tq = q_ref.shape[1]
    q = q_ref[0] * jnp.asarray(A_HEAD_DIM ** -0.5, BF16)
    q1 = q[:, :A_HEAD_DIM]
    q2 = q[:, A_HEAD_DIM:]

    def scores(j):
        off = pl.multiple_of(j * tq, tq)
        k = k_ref[0, pl.ds(off, tq), :]
        v = v_ref[0, pl.ds(off, tq), :]
        return _dot_nt(q1, k[:, :A_HEAD_DIM]), _dot_nt(q2, k[:, A_HEAD_DIM:]), v

    def update(carry, s, v):
        m, l, acc = carry
        m_new = jnp.maximum(m, jnp.max(s, -1, keepdims=True))
        alpha = jnp.exp(m - m_new)
        p = jnp.exp(s - m_new)
        l = alpha * l + jnp.sum(p, -1, keepdims=True)
        acc = alpha * acc + _dot(p.astype(BF16), v)
        return m_new, l, acc

    def body(j, carry):
        s1, s2, v = scores(j)
        return update(carry[0], s1, v), update(carry[1], s2, v)

    init = (jnp.full((tq, 1), NEG_INF, F32), jnp.zeros((tq, 1), F32), jnp.zeros((tq, A_VAL_DIM), F32))
    c1, c2 = lax.fori_loop(0, i, body, (init, init))

    s1, s2, v = scores(i)
    row = lax.broadcasted_iota(jnp.int32, (tq, tq), 0)
    col = lax.broadcasted_iota(jnp.int32, (tq, tq), 1)
    allowed = (col // CHUNK) <= (row // CHUNK)
    c1 = update(c1, jnp.where(allowed, s1, NEG_INF), v)
    c2 = update(c2, jnp.where(allowed, s2, NEG_INF), v)

    lam_rows = lam_ref[...]
    lam = (jnp.exp(jnp.sum(lam_rows[0:1] * lam_rows[1:2], -1, keepdims=True))
           - jnp.exp(jnp.sum(lam_rows[2:3] * lam_rows[3:4], -1, keepdims=True)) + lam_init)
    o = c1[2] / c1[1] - lam * (c2[2] / c2[1])
    o = o * lax.rsqrt(jnp.mean(o * o, -1, keepdims=True) + RMS_EPS) * g_ref[...]
    o_ref[0] = (o * (1.0 - lam_init)).astype(o_ref.dtype)


def _diff_attn(qk, v, lam_rows, subln_g, lam_init):
    b, s, _ = v.shape
    tq = ATTN_TILE
    k_col0 = A_HEADS
    return pl.pallas_call(
        functools.partial(_diff_attn_kernel, lam_init=lam_init),
        grid=(b, A_HEADS, s // tq),
        in_specs=[
            pl.BlockSpec((4, A_HEAD_DIM), lambda bi, h, i: (0, 0)),
            pl.BlockSpec((1, A_VAL_DIM), lambda bi, h, i: (0, 0)),
            pl.BlockSpec((1, tq, A_VAL_DIM), lambda bi, h, i: (bi, i, h)),
            pl.BlockSpec((1, s, A_VAL_DIM), lambda bi, h, i: (bi, 0, k_col0 + h)),
            pl.BlockSpec((1, s, A_VAL_DIM), lambda bi, h, i: (bi, 0, h)),
        ],
        out_specs=pl.BlockSpec((1, tq, A_VAL_DIM), lambda bi, h, i: (bi, i, h)),
        out_shape=jax.ShapeDtypeStruct(v.shape, BF16),
        compiler_params=_params("parallel", "parallel", "arbitrary"),
        name="diff_attn",
    )(lam_rows, subln_g.reshape(1, A_VAL_DIM), qk, qk, v)


def _pool_kernel(p_ref, w_ref, scale_ref, o_ref):
    g = pl.program_id(1)
    x = p_ref[0]
    row = lax.broadcasted_iota(jnp.int32, x.shape, 0)
    for gi, window in enumerate(POOL_WINDOWS):
        @pl.when(g == gi)
        def _(window=window):
            ws = x
            shift = 1
            while shift < window:
                ws = ws + jnp.where(row >= shift, pltpu.roll(ws, shift, 0), 0.0)
                shift *= 2
            count = jnp.minimum(row + 1, window).astype(F32)
            d = ws / count - x
            y = _dot(d.astype(BF16), w_ref[0]) * scale_ref[...]
            o_ref[0] = y.astype(o_ref.dtype)


def _pool(u_pool, pool_w16, pool_scale):
    b, s, d = u_pool.shape
    cg = POOL_GROUP_DIM
    return pl.pallas_call(
        _pool_kernel,
        grid=(b, len(POOL_WINDOWS)),
        in_specs=[
            pl.BlockSpec((1, s, cg), lambda bi, g: (bi, 0, g)),
            pl.BlockSpec((1, cg, cg), lambda bi, g: (g, 0, 0)),
            pl.BlockSpec((1, cg), lambda bi, g: (0, g)),
        ],
        out_specs=pl.BlockSpec((1, s, cg), lambda bi, g: (bi, 0, g)),
        out_shape=jax.ShapeDtypeStruct((b, s, d), BF16),
        compiler_params=_params("parallel", "parallel"),
        name="pool_mixer",
    )(u_pool, pool_w16, pool_scale.reshape(1, d))


def _block_mask():
    r = lax.broadcasted_iota(jnp.int32, (RW_QUAD, RW_QUAD), 0)
    c = lax.broadcasted_iota(jnp.int32, (RW_QUAD, RW_QUAD), 1)
    return (r // R_HEAD_DIM) == (c // R_HEAD_DIM)


def _bd(x, mask):
    return jnp.where(mask, jnp.concatenate([x] * (RW_QUAD // R_HEAD_DIM), axis=0), 0.0)


def _seg_sum(x, ones_bd):
    parts = [_dot(x[:, q * RW_QUAD:(q + 1) * RW_QUAD], ones_bd, HIGHEST) for q in range(x.shape[1] // RW_QUAD)]
    return jnp.concatenate(parts, axis=1)


def _chunk_quad(r, k, v, a, b, lw, cum, mask):
    c = RW_CHUNK
    cum_end = cum[c - 1:c, :]
    at = a * jnp.exp(cum - lw)
    rt = r * jnp.exp(cum)
    e_neg = jnp.exp(-cum)
    bt = b * e_neg
    kt = k * e_neg
    e_end = jnp.exp(cum_end - cum)
    bh = b * e_end
    kh = k * e_end

    lhs = jnp.concatenate([at, rt], axis=0)
    rhs = jnp.concatenate([_bd(bt, mask), _bd(kt, mask)], axis=0)
    sc = _dot_nt(lhs, rhs, HIGHEST)
    ii = lax.broadcasted_iota(jnp.int32, (c, RW_QUAD), 0)
    jj = lax.broadcasted_iota(jnp.int32, (c, RW_QUAD), 1) % c
    strict = ii > jj
    incl = ii >= jj
    a_ab = jnp.where(strict, sc[:c, :RW_QUAD], 0.0)
    a_ak = jnp.where(strict, sc[:c, RW_QUAD:], 0.0)
    a_rb = jnp.where(incl, sc[c:, :RW_QUAD], 0.0)
    a_rk = jnp.where(incl, sc[c:, RW_QUAD:], 0.0)

    av = _dot(jnp.concatenate([a_ak, a_rk], axis=0), _bd(v, mask), HIGHEST)
    xa = at
    xu = av[:c]
    pk = a_ab
    n_steps = int(math.log2(c))
    for step in range(n_steps):
        xa = xa + _dot(pk, _bd(xa, mask), HIGHEST)
        xu = xu + _dot(pk, _bd(xu, mask), HIGHEST)
        if step + 1 < n_steps:
            pk = _dot(pk, _bd(pk, mask), HIGHEST)

    rp = rt + _dot(a_rb, _bd(xa, mask), HIGHEST)
    y0 = av[c:] + _dot(a_rb, _bd(xu, mask), HIGHEST)
    z1 = jnp.where(mask, _dot_tn(bh, xa, HIGHEST), 0.0)
    z2 = jnp.where(mask, _dot_tn(bh, xu, HIGHEST) + _dot_tn(kh, v, HIGHEST), 0.0)

    def fold(z):
        out = z[:c]
        for h in range(1, RW_QUAD // R_HEAD_DIM):
            out = out + z[h * c:(h + 1) * c]
        return out

    m = fold(z1) + jnp.where(ii == jj, jnp.exp(cum_end), 0.0)
    n = fold(z2)
    return rp, y0, m, n


def _rwkv_prep_kernel(rkv_ref, rkv_halo_ref, lora_ref, lora_halo_ref, mu_rkv_ref, mu_lora_ref, vec_ref,
                      w2_ref, a2_ref, g2_ref, rp_ref, y0_ref, m_ref, n_ref, bonus_ref, gate_ref):
    t = pl.program_id(1)
    ts = rkv_ref.shape[1]
    d = D_MODEL

    def shifted(x, halo, mu):
        row = lax.broadcasted_iota(jnp.int32, x.shape, 0)
        before = jnp.where(t > 0, halo[V7X_SUBLANES - 1:V7X_SUBLANES, :], 0.0)
        prev = jnp.where(row == 0, before, pltpu.roll(x, 1, 0))
        return x + (prev - x) * mu

    xs = shifted(rkv_ref[0], rkv_halo_ref[0], mu_rkv_ref[...])
    ls = shifted(lora_ref[0], lora_halo_ref[0], mu_lora_ref[...])
    r = xs[:, :d]
    k = xs[:, d:2 * d]
    v = xs[:, 2 * d:]
    vec = vec_ref[...]
    w0, a0, k_k, k_a, r_k = vec[0:1], vec[1:2], vec[2:3], vec[3:4], vec[4:5]

    z = w0 + _dot(jnp.tanh(ls[:, :128]).astype(BF16), w2_ref[...])
    lw = -math.exp(-0.5) * jax.nn.sigmoid(z)
    a_sig = jax.nn.sigmoid(a0 + _dot(ls[:, 128:256].astype(BF16), a2_ref[...]))
    gate_ref[0] = _dot(jax.nn.sigmoid(ls[:, 256:]).astype(BF16), g2_ref[...])

    mask = _block_mask()
    ones_bd = mask.astype(F32)
    kk = k * k_k
    kk = kk / jnp.maximum(jnp.sqrt(_seg_sum(kk * kk, ones_bd)), 1e-12)
    k2 = k * (1.0 + (a_sig - 1.0) * k_a)
    a_vec = -kk
    b_vec = kk * a_sig
    bonus_ref[0] = _seg_sum(r * k2 * r_k, ones_bd) * v

    row_in_chunk = lax.broadcasted_iota(jnp.int32, lw.shape, 0) % RW_CHUNK
    cum = lw
    shift = 1
    while shift < RW_CHUNK:
        cum = cum + jnp.where(row_in_chunk >= shift, pltpu.roll(cum, shift, 0), 0.0)
        shift *= 2

    for c in range(ts // RW_CHUNK):
        rows = slice(c * RW_CHUNK, (c + 1) * RW_CHUNK)
        for q in range(d // RW_QUAD):
            cols = slice(q * RW_QUAD, (q + 1) * RW_QUAD)
            rp, y0, m, n = _chunk_quad(r[rows, cols], k2[rows, cols], v[rows, cols], a_vec[rows, cols],
                                       b_vec[rows, cols], lw[rows, cols], cum[rows, cols], mask)
            rp_ref[0, rows, cols] = rp
            y0_ref[0, rows, cols] = y0
            m_ref[0, rows, cols] = m
            n_ref[0, rows, cols] = n


def _rwkv_prep(u_rkv, u_lora, mu_rkv, mu_lora, vecs, w2p, a2p, g2p):
    b, s, w3 = u_rkv.shape
    d = D_MODEL
    ts = _tile(s, RW_PREP_TILE)
    halo_blocks = ts // V7X_SUBLANES

    def tile_spec(width):
        return pl.BlockSpec((1, ts, width), lambda bi, t: (bi, t, 0))

    def halo_spec(width):
        return pl.BlockSpec((1, V7X_SUBLANES, width), lambda bi, t: (bi, jnp.maximum(t * halo_blocks - 1, 0), 0))

    def const_spec(shape):
        return pl.BlockSpec(shape, lambda bi, t: (0,) * len(shape))

    out = jax.ShapeDtypeStruct((b, s, d), F32)
    return pl.pallas_call(
        _rwkv_prep_kernel,
        grid=(b, s // ts),
        in_specs=[tile_spec(w3), halo_spec(w3), tile_spec(LORA_PAD), halo_spec(LORA_PAD),
                  const_spec((1, w3)), const_spec((1, LORA_PAD)), const_spec(vecs.shape),
                  const_spec(w2p.shape), const_spec(a2p.shape), const_spec(g2p.shape)],
        out_specs=[tile_spec(d)] * 6,
        out_shape=[out] * 6,
        compiler_params=_params("parallel", "parallel"),
        name="rwkv_prep",
    )(u_rkv, u_rkv, u_lora, u_lora, mu_rkv, mu_lora, vecs, w2p, a2p, g2p)


def _rwkv_scan_kernel(rp_ref, y0_ref, m_ref, n_ref, bonus_ref, gate_ref, vec_ref, o_ref, state_ref):
    @pl.when(pl.program_id(1) == 0)
    def _():
        state_ref[...] = jnp.zeros_like(state_ref)

    ts = rp_ref.shape[1]
    mask = _block_mask()
    inv_n = 1.0 / R_HEAD_DIM
    ones_bd = mask.astype(F32)
    vec = vec_ref[...]
    lnx_g, lnx_b = vec[5:6], vec[6:7]
    for q in range(D_MODEL // RW_QUAD):
        cols = slice(q * RW_QUAD, (q + 1) * RW_QUAD)
        state = state_ref[:, cols]
        for c in range(ts // RW_CHUNK):
            rows = slice(c * RW_CHUNK, (c + 1) * RW_CHUNK)
            lhs = jnp.concatenate([rp_ref[0, rows, cols], m_ref[0, rows, cols]], axis=0)
            out = _dot(lhs, _bd(state, mask), HIGHEST)
            y = out[:RW_CHUNK] + y0_ref[0, rows, cols]
            state = out[RW_CHUNK:] + n_ref[0, rows, cols]
            mu = _dot(y, ones_bd, HIGHEST) * inv_n
            dy = y - mu
            var = _dot(dy * dy, ones_bd, HIGHEST) * inv_n
            yn = dy * lax.rsqrt(var + RWKV_GN_EPS) * lnx_g[:, cols] + lnx_b[:, cols]
            o_ref[0, rows, cols] = ((yn + bonus_ref[0, rows, cols]) * gate_ref[0, rows, cols]).astype(o_ref.dtype)
        state_ref[:, cols] = state


def _rwkv_scan(rp, y0, m, n, bonus, gate, vecs):
    b, s, d = rp.shape
    ts = _tile(s, RW_SCAN_TILE)
    tile_spec = pl.BlockSpec((1, ts, d), lambda bi, t: (bi, t, 0))
    return pl.pallas_call(
        _rwkv_scan_kernel,
        grid=(b, s // ts),
        in_specs=[tile_spec] * 6 + [pl.BlockSpec(vecs.shape, lambda bi, t: (0, 0))],
        out_specs=tile_spec,
        out_shape=jax.ShapeDtypeStruct((b, s, d), BF16),
        scratch_shapes=[pltpu.VMEM((R_HEAD_DIM, d), F32)],
        compiler_params=_params("parallel", "arbitrary"),
        name="rwkv_scan",
    )(rp, y0, m, n, bonus, gate, vecs)


def _merge_kernel(ya_ref, yb_ref, yc_ref, ug_ref, bg_ref, h_ref, wa_ref, wb_ref, wc_ref, wo_ref,
                  g_ref, b_ref, h32_ref, h16_ref):
    d = D_MODEL
    ug = ug_ref[...]
    bg = bg_ref[...]
    merged = jnp.zeros((ya_ref.shape[0], d), F32)
    for idx, (y_ref, w_ref) in enumerate(((ya_ref, wa_ref), (yb_ref, wb_ref), (yc_ref, wc_ref))):
        gate = jax.nn.sigmoid(ug[:, idx * d:(idx + 1) * d] + bg[idx:idx + 1, :])
        merged = merged + gate * _dot(y_ref[...], w_ref[...])
    z = DN_ALPHA * h_ref[...] + _dot(merged.astype(BF16), wo_ref[...])
    h = _ln(z, g_ref[...], b_ref[...], LN_EPS)
    h32_ref[...] = h
    h16_ref[...] = h.astype(BF16)


def _merge(ya, yb, yc, u_gate, b_gate, h32, wa, wb, wc, wo, ln_g, ln_b):
    t, d = h32.shape
    tm = _tile(t, FUSED_ROW_TILE)
    row = pl.BlockSpec((tm, d), lambda i: (i, 0))
    wspec = pl.BlockSpec((d, d), lambda i: (0, 0))
    vec = pl.BlockSpec((1, d), lambda i: (0, 0))
    return pl.pallas_call(
        _merge_kernel,
        grid=(t // tm,),
        in_specs=[row, row, row, pl.BlockSpec((tm, 3 * d), lambda i: (i, 0)),
                  pl.BlockSpec((8, d), lambda i: (0, 0)), row, wspec, wspec, wspec, wspec, vec, vec],
        out_specs=[row, row],
        out_shape=[jax.ShapeDtypeStruct((t, d), F32), jax.ShapeDtypeStruct((t, d), BF16)],
        compiler_params=_params("parallel"),
        name="merge_out",
    )(ya, yb, yc, u_gate, b_gate, h32, wa, wb, wc, wo, ln_g.reshape(1, d), ln_b.reshape(1, d))


def _cross_kernel(h16_ref, h32_ref, kv_ref, wq_ref, wo_ref, g_ref, b_ref, o32_ref, o16_ref):
    d = D_MODEL
    q = _dot(h16_ref[0], wq_ref[...]).astype(BF16)
    kv = kv_ref[0]
    heads = []
    for hd in range(X_HEADS):
        cols = slice(hd * X_HEAD_DIM, (hd + 1) * X_HEAD_DIM)
        s = _dot_nt(q[:, cols], kv[:, cols]) * (X_HEAD_DIM ** -0.5)
        p = jnp.exp(s - jnp.max(s, -1, keepdims=True))
        o = _dot(p.astype(BF16), kv[:, d + hd * X_HEAD_DIM:d + (hd + 1) * X_HEAD_DIM])
        heads.append((o / jnp.sum(p, -1, keepdims=True)).astype(BF16))
    attn = jnp.concatenate(heads, axis=1)
    z = DN_ALPHA * h32_ref[0] + _dot(attn, wo_ref[...])
    h = _ln(z, g_ref[...], b_ref[...], LN_EPS)
    o32_ref[0] = h
    o16_ref[0] = h.astype(BF16)


def _cross(h16, h32, kv, wq, wo, ln_g, ln_b):
    b, s, d = h32.shape
    tm = _tile(s, FUSED_ROW_TILE)
    row = pl.BlockSpec((1, tm, d), lambda bi, i: (bi, i, 0))
    wspec = pl.BlockSpec((d, d), lambda bi, i: (0, 0))
    vec = pl.BlockSpec((1, d), lambda bi, i: (0, 0))
    n_mem = kv.shape[1]
    return pl.pallas_call(
        _cross_kernel,
        grid=(b, s // tm),
        in_specs=[row, row, pl.BlockSpec((1, n_mem, 2 * d), lambda bi, i: (bi, 0, 0)), wspec, wspec, vec, vec],
        out_specs=[row, row],
        out_shape=[jax.ShapeDtypeStruct((b, s, d), F32), jax.ShapeDtypeStruct((b, s, d), BF16)],
        compiler_params=_params("parallel", "parallel"),
        name="cross_attn",
    )(h16, h32, kv, wq, wo, ln_g.reshape(1, d), ln_b.reshape(1, d))


def _mlp_kernel(h16_ref, h32_ref, w1_ref, w2_ref, g_ref, b_ref, o32_ref, o16_ref, acc_ref):
    f = pl.program_id(1)

    @pl.when(f == 0)
    def _():
        acc_ref[...] = jnp.zeros_like(acc_ref)

    a = jnp.maximum(_dot(h16_ref[...], w1_ref[...]), 0.0)
    acc_ref[...] += _dot((a * a).astype(BF16), w2_ref[...])

    @pl.when(f == pl.num_programs(1) - 1)
    def _():
        h = _ln(DN_ALPHA * h32_ref[...] + acc_ref[...], g_ref[...], b_ref[...], LN_EPS)
        o32_ref[...] = h
        o16_ref[...] = h.astype(BF16)


def _mlp(h16, h32, w1, w2, ln_g, ln_b):
    t, d = h32.shape
    dff = w1.shape[1]
    tm = _tile(t, MLP_ROW_TILE)
    tf = _tile(dff, MLP_FF_TILE)
    row = pl.BlockSpec((tm, d), lambda i, f: (i, 0))
    vec = pl.BlockSpec((1, d), lambda i, f: (0, 0))
    return pl.pallas_call(
        _mlp_kernel,
        grid=(t // tm, dff // tf),
        in_specs=[row, row, pl.BlockSpec((d, tf), lambda i, f: (0, f)), pl.BlockSpec((tf, d), lambda i, f: (f, 0)),
                  vec, vec],
        out_specs=[row, row],
        out_shape=[jax.ShapeDtypeStruct((t, d), F32), jax.ShapeDtypeStruct((t, d), BF16)],
        scratch_shapes=[pltpu.VMEM((tm, d), F32)],
        compiler_params=_params("parallel", "arbitrary"),
        name="mlp",
    )(h16, h32, w1, w2, ln_g.reshape(1, d), ln_b.reshape(1, d))


def _pad_rows(w, rows):
    return jnp.pad(w, ((0, rows - w.shape[0]), (0, 0)))


def _lora_layout(w):
    pad = lambda t, n: jnp.pad(t, [(0, 0)] * (t.ndim - 1) + [(0, n - t.shape[-1])])
    xw = w[..., :DECAY_LORA]
    xa = w[..., DECAY_LORA:DECAY_LORA + AAA_LORA]
    xg = w[..., DECAY_LORA + AAA_LORA:]
    return jnp.concatenate([pad(xw, 128), pad(xa, 128), pad(xg, 256)], axis=-1)


def kernel(x, mem, positions, ln_in_g, ln_in_b, w_in, b_gate, lam_q1, lam_k1, lam_q2, lam_k2, attn_subln_g, w_br_attn, pool_w, pool_scale, w_br_pool, rwkv_mu, rwkv_w0, rwkv_w2, rwkv_a0, rwkv_a2, rwkv_g2, rwkv_k_k, rwkv_k_a, rwkv_r_k, rwkv_lnx_g, rwkv_lnx_b, w_br_rwkv, w_out, ln1_g, ln1_b, w_xq, w_xkv, w_xo, ln2_g, ln2_b, w_ff1, w_ff2, ln3_g, ln3_b):
    bsz, seq, d = x.shape
    t = bsz * seq
    n_mem = mem.shape[1]
    depth = w_in.shape[0]

    inv_freq = 1.0 / (ROPE_THETA ** (jnp.arange(0, A_HEAD_DIM, 2, dtype=F32) / A_HEAD_DIM))
    ang = positions.astype(F32).reshape(t, 1) * inv_freq
    cos, sin = jnp.cos(ang), jnp.sin(ang)
    cos_t = jnp.concatenate([cos, cos, cos, cos], axis=-1)
    sin_t = jnp.concatenate([-sin, sin, -sin, sin], axis=-1)

    c_qk = 2 * A_HEADS * 2 * A_HEAD_DIM
    c_v = c_qk + A_HEADS * A_VAL_DIM
    c_pool = c_v + d
    c_rkv = c_pool + 3 * d
    c_lora = c_rkv + DECAY_LORA + AAA_LORA + GATE_LORA

    h32, h16 = _ln_in(x.reshape(t, d), ln_in_g, ln_in_b)
    mem2d = mem.reshape(bsz * n_mem, d)

    for l in range(depth):
        w = w_in[l]
        w_qk = w[:, :c_qk].astype(BF16)
        w_v = w[:, c_qk:c_v].astype(BF16)
        w_pool = w[:, c_v:c_pool].astype(BF16)
        w_rkv = w[:, c_pool:c_rkv].astype(BF16)
        w_lora = _lora_layout(w[:, c_rkv:c_lora]).astype(BF16)
        w_gate = w[:, c_lora:].astype(BF16)

        qk = _proj(h16, w_qk, BF16, rope=(cos_t, sin_t), name="proj_qk").reshape(bsz, seq, c_qk)
        v = _proj(h16, w_v, BF16, name="proj_v").reshape(bsz, seq, d)
        u_pool = _proj(h16, w_pool, F32, name="proj_pool").reshape(bsz, seq, d)
        u_rkv = _proj(h16, w_rkv, F32, name="proj_rkv").reshape(bsz, seq, 3 * d)
        u_lora = _proj(h16, w_lora, F32, name="proj_lora").reshape(bsz, seq, LORA_PAD)
        u_gate = _proj(h16, w_gate, F32, name="proj_gate")

        lam_init = 0.8 - 0.6 * math.exp(-0.3 * l)
        lam_rows = jnp.stack([lam_q1[l], lam_k1[l], lam_q2[l], lam_k2[l]]).astype(F32)
        y_a = _diff_attn(qk, v, lam_rows, attn_subln_g[l], lam_init)

        y_b = _pool(u_pool, pool_w[l].astype(BF16), pool_scale[l])

        mu = rwkv_mu[l]
        mu_rkv = mu[:3 * d].reshape(1, 3 * d)
        mu_lora = _lora_layout(mu[3 * d:]).reshape(1, LORA_PAD)
        vecs = jnp.stack([rwkv_w0[l], rwkv_a0[l], rwkv_k_k[l], rwkv_k_a[l], rwkv_r_k[l].reshape(d),
                          rwkv_lnx_g[l], rwkv_lnx_b[l], jnp.zeros((d,), F32)])
        w2p = _pad_rows(rwkv_w2[l], 128).astype(BF16)
        a2p = _pad_rows(rwkv_a2[l], 128).astype(BF16)
        g2p = _pad_rows(rwkv_g2[l], 256).astype(BF16)
        rp, y0, m_c, n_c, bonus, gate = _rwkv_prep(u_rkv, u_lora, mu_rkv, mu_lora, vecs, w2p, a2p, g2p)
        y_c = _rwkv_scan(rp, y0, m_c, n_c, bonus, gate, vecs)

        bg = jnp.pad(b_gate[l], ((0, 8 - b_gate.shape[1]), (0, 0)))
        h32, h16 = _merge(y_a.reshape(t, d), y_b.reshape(t, d), y_c.reshape(t, d), u_gate, bg, h32,
                          w_br_attn[l].astype(BF16), w_br_pool[l].astype(BF16), w_br_rwkv[l].astype(BF16),
                          w_out[l].astype(BF16), ln1_g[l], ln1_b[l])

        kv = _proj(mem2d, w_xkv[l].astype(BF16), BF16, name="proj_kv").reshape(bsz, n_mem, 2 * d)
        h32, h16 = _cross(h16.reshape(bsz, seq, d), h32.reshape(bsz, seq, d), kv, w_xq[l].astype(BF16),
                          w_xo[l].astype(BF16), ln2_g[l], ln2_b[l])
        h32, h16 = h32.reshape(t, d), h16.reshape(t, d)

        h32, h16 = _mlp(h16, h32, w_ff1[l].astype(BF16), w_ff2[l].astype(BF16), ln3_g[l], ln3_b[l])

    return h32.reshape(bsz, seq, d)
```

```python
import functools
import math

import jax
import jax.numpy as jnp
from jax import lax
from jax.experimental import pallas as pl
from jax.experimental.pallas import tpu as pltpu

F32 = jnp.float32
BF16 = jnp.bfloat16

D_MODEL = 1024
DEPTH = 2
CHUNK = 64
A_HEADS = 8
A_HEAD_DIM = 64
A_VAL_DIM = 2 * A_HEAD_DIM
ROPE_THETA = 10000.0
POOL_WINDOWS = (2, 4, 8, 16)
POOL_GROUP_DIM = D_MODEL // len(POOL_WINDOWS)
R_HEAD_DIM = 64
DECAY_LORA = 64
AAA_LORA = 64
GATE_LORA = 160
X_HEADS = 4
X_HEAD_DIM = D_MODEL // X_HEADS
D_FF = 4 * D_MODEL
DN_ALPHA = (2 * DEPTH) ** 0.25
LN_EPS = 1e-5
RMS_EPS = 1e-5
RWKV_GN_EPS = 64e-5
NEG_INF = -1e30

V7X_LANES = 128
V7X_SUBLANES = 8
V7X_MXU_DIM = 256
V7X_VMEM_LIMIT_BYTES = 56 * 1024 * 1024

ROW_TILE = 1024
COL_TILE = 512
FUSED_ROW_TILE = 256
MLP_ROW_TILE = 512
MLP_FF_TILE = 1024
ATTN_Q_TILE = 256
ATTN_K_TILE = 256
RW_CHUNK = 64
RW_QUAD = V7X_MXU_DIM
RW_PREP_TILE = 128
RW_SCAN_TILE = 256
LORA_PAD = 512
HIGHEST = lax.Precision.HIGHEST
RW_PASSES = 1
RW_SCAN_PASSES = 3


def _tile(n, pref):
    return pref if n % pref == 0 else n


def _params(*sem):
    return pltpu.CompilerParams(dimension_semantics=sem, vmem_limit_bytes=V7X_VMEM_LIMIT_BYTES)


def _ln(z, g, b, eps):
    mu = jnp.mean(z, -1, keepdims=True)
    d = z - mu
    var = jnp.mean(d * d, -1, keepdims=True)
    return d * lax.rsqrt(var + eps) * g + b


def _dot(a, b, precision=None):
    return jnp.dot(a, b, preferred_element_type=F32, precision=precision)


def _dot_nt(a, b, precision=None):
    return lax.dot_general(a, b, (((1,), (1,)), ((), ())), preferred_element_type=F32, precision=precision)


def _dot_tn(a, b, precision=None):
    return lax.dot_general(a, b, (((0,), (0,)), ((), ())), preferred_element_type=F32, precision=precision)


_DIMS = {"nn": (((1,), (0,)), ((), ())), "nt": (((1,), (1,)), ((), ())), "tn": (((0,), (0,)), ((), ()))}


def _split(x):
    hi = x.astype(BF16)
    return hi, (x - hi.astype(F32)).astype(BF16)


def _mm(a, b, kind="nn", passes=None):
    passes = RW_PASSES if passes is None else passes
    dg = functools.partial(lax.dot_general, dimension_numbers=_DIMS[kind], preferred_element_type=F32)
    if passes == 6:
        return dg(a, b, precision=HIGHEST)
    if passes == 1:
        return dg(a.astype(BF16), b.astype(BF16))
    a_hi, a_lo = _split(a)
    b_hi, b_lo = _split(b)
    return dg(a_hi, b_hi) + (dg(a_hi, b_lo) + dg(a_lo, b_hi))


def _ln_in_kernel(x_ref, g_ref, b_ref, h32_ref, h16_ref):
    h = _ln(x_ref[...], g_ref[...], b_ref[...], LN_EPS)
    h32_ref[...] = h
    h16_ref[...] = h.astype(BF16)


def _ln_in(x2d, g, b):
    t, d = x2d.shape
    tm = _tile(t, ROW_TILE)
    row = pl.BlockSpec((tm, d), lambda i: (i, 0))
    vec = pl.BlockSpec((1, d), lambda i: (0, 0))
    return pl.pallas_call(
        _ln_in_kernel,
        grid=(t // tm,),
        in_specs=[row, vec, vec],
        out_specs=[row, row],
        out_shape=[jax.ShapeDtypeStruct((t, d), F32), jax.ShapeDtypeStruct((t, d), BF16)],
        compiler_params=_params("parallel"),
        name="ln_in",
    )(x2d, g.reshape(1, d), b.reshape(1, d))


def _proj_kernel(x_ref, w_ref, o_ref):
    o_ref[...] = _dot(x_ref[...].astype(BF16), w_ref[...]).astype(o_ref.dtype)


def _proj_rope_kernel(x_ref, w_ref, cos_ref, sin_ref, o_ref):
    t = _dot(x_ref[...], w_ref[...])
    cos = cos_ref[...]
    sin = sin_ref[...]
    lane = lax.broadcasted_iota(jnp.int32, cos.shape, 1)
    first_half = (lane & (A_HEAD_DIM - 1)) < A_HEAD_DIM // 2
    for j in range(t.shape[1] // V7X_LANES):
        blk = t[:, j * V7X_LANES:(j + 1) * V7X_LANES]
        partner = jnp.where(first_half,
                            pltpu.roll(blk, V7X_LANES - A_HEAD_DIM // 2, 1),
                            pltpu.roll(blk, A_HEAD_DIM // 2, 1))
        o_ref[:, j * V7X_LANES:(j + 1) * V7X_LANES] = (blk * cos + partner * sin).astype(o_ref.dtype)


def _proj(x, w16, out_dtype, rope=None, name="proj"):
    m, k = x.shape
    n = w16.shape[1]
    tm = _tile(m, ROW_TILE)
    tn = _tile(n, COL_TILE)
    in_specs = [pl.BlockSpec((tm, k), lambda i, j: (i, 0)), pl.BlockSpec((k, tn), lambda i, j: (0, j))]
    args = [x, w16]
    body = _proj_kernel
    if rope is not None:
        in_specs += [pl.BlockSpec((tm, V7X_LANES), lambda i, j: (i, 0))] * 2
        args += list(rope)
        body = _proj_rope_kernel
    return pl.pallas_call(
        body,
        grid=(m // tm, n // tn),
        in_specs=in_specs,
        out_specs=pl.BlockSpec((tm, tn), lambda i, j: (i, j)),
        out_shape=jax.ShapeDtypeStruct((m, n), out_dtype),
        compiler_params=_params("parallel", "parallel"),
        name=name,
    )(*args)


def _diff_attn_kernel(lam_ref, g_ref, q_ref, k_ref, v_ref, o_ref, s_buf, p_buf, *, lam_init):
    i = pl.program_id(2)
    tq = q_ref.shape[1]
    tk = min(ATTN_K_TILE, k_ref.shape[1])
    q = q_ref[0] * jnp.asarray(A_HEAD_DIM ** -0.5, BF16)
    lane = lax.broadcasted_iota(jnp.int32, q.shape, 1)
    zero = jnp.zeros_like(q)
    qs = jnp.concatenate([jnp.where(lane < A_HEAD_DIM, q, zero), jnp.where(lane >= A_HEAD_DIM, q, zero)], axis=0)

    def keys(j):
        return k_ref[0, pl.ds(pl.multiple_of(j * tk, tk), tk), :]

    def values(j):
        return v_ref[0, pl.ds(pl.multiple_of(j * tk, tk), tk), :]

    def prev_values_product(j):
        return _dot_tn(values(jnp.maximum(j - 1, 0)), p_buf[1 - (j & 1)])

    def softmax_step(j, carry, s, pv, masked):
        m, l, acc = carry
        if masked:
            kpos = j * tk + lax.broadcasted_iota(jnp.int32, (tk, 2 * tq), 0)
            qpos = i * tq + (lax.broadcasted_iota(jnp.int32, (tk, 2 * tq), 1) & (tq - 1))
            chunk_shift = CHUNK.bit_length() - 1
            s = jnp.where((kpos >> chunk_shift) <= (qpos >> chunk_shift), s, NEG_INF)
        m_new = jnp.maximum(m, jnp.max(s, axis=0, keepdims=True))
        alpha = jnp.exp(m - m_new)
        p = jnp.exp(s - m_new)
        l = alpha * l + jnp.sum(p.reshape(tk // V7X_SUBLANES, V7X_SUBLANES, 2 * tq), axis=0)
        return (m_new, l, alpha * (acc + pv)), p.astype(BF16)

    def body(j, carry):
        s = s_buf[j & 1]
        pv = prev_values_product(j)
        s_next = _dot_nt(keys(j + 1), qs)
        carry, p = softmax_step(j, carry, s, pv, False)
        s_buf[1 - (j & 1)] = s_next
        p_buf[j & 1] = p
        return carry

    s_buf[0] = _dot_nt(keys(0), qs)
    p_buf[1] = jnp.zeros(p_buf.shape[1:], BF16)
    init = (jnp.full((1, 2 * tq), NEG_INF, F32), jnp.zeros((V7X_SUBLANES, 2 * tq), F32),
            jnp.zeros((A_VAL_DIM, 2 * tq), F32))
    n_full = (i * tq) // tk
    carry = lax.fori_loop(0, n_full, body, init)
    (_, l, acc), p = softmax_step(n_full, carry, s_buf[n_full & 1], prev_values_product(n_full), True)
    acc = acc + _dot_tn(values(n_full), p)
    out = acc / jnp.sum(l, axis=0, keepdims=True)

    lam_rows = lam_ref[...]
    lam = (jnp.exp(jnp.sum(lam_rows[0:1] * lam_rows[1:2], -1, keepdims=True))
           - jnp.exp(jnp.sum(lam_rows[2:3] * lam_rows[3:4], -1, keepdims=True)) + lam_init)
    o = out[:, :tq] - lam * out[:, tq:]
    o = o * lax.rsqrt(jnp.mean(o * o, axis=0, keepdims=True) + RMS_EPS) * g_ref[...]
    o_ref[0] = (o * (1.0 - lam_init)).T.astype(o_ref.dtype)


def _diff_attn(qk, v, lam_rows, subln_g, lam_init):
    b, s, _ = v.shape
    tq = _tile(s, ATTN_Q_TILE)
    tk = min(ATTN_K_TILE, s)
    k_col0 = A_HEADS
    return pl.pallas_call(
        functools.partial(_diff_attn_kernel, lam_init=lam_init),
        grid=(b, A_HEADS, s // tq),
        in_specs=[
            pl.BlockSpec((4, A_HEAD_DIM), lambda bi, h, i: (0, 0)),
            pl.BlockSpec((A_VAL_DIM, 1), lambda bi, h, i: (0, 0)),
            pl.BlockSpec((1, tq, A_VAL_DIM), lambda bi, h, i: (bi, i, h)),
            pl.BlockSpec((1, s, A_VAL_DIM), lambda bi, h, i: (bi, 0, k_col0 + h)),
            pl.BlockSpec((1, s, A_VAL_DIM), lambda bi, h, i: (bi, 0, h)),
        ],
        out_specs=pl.BlockSpec((1, tq, A_VAL_DIM), lambda bi, h, i: (bi, i, h)),
        out_shape=jax.ShapeDtypeStruct(v.shape, BF16),
        scratch_shapes=[pltpu.VMEM((2, tk, 2 * tq), F32), pltpu.VMEM((2, tk, 2 * tq), BF16)],
        compiler_params=_params("parallel", "parallel", "arbitrary"),
        name="diff_attn",
    )(lam_rows, subln_g.reshape(A_VAL_DIM, 1), qk, qk, v)


def _pool_kernel(p_ref, w_ref, scale_ref, o_ref):
    g = pl.program_id(1)
    x = p_ref[0]
    row = lax.broadcasted_iota(jnp.int32, x.shape, 0)
    for gi, window in enumerate(POOL_WINDOWS):
        @pl.when(g == gi)
        def _(window=window):
            ws = x
            shift = 1
            while shift < window:
                ws = ws + jnp.where(row >= shift, pltpu.roll(ws, shift, 0), 0.0)
                shift *= 2
            count = jnp.minimum(row + 1, window).astype(F32)
            d = ws / count - x
            y = _dot(d.astype(BF16), w_ref[0]) * scale_ref[...]
            o_ref[0] = y.astype(o_ref.dtype)


def _pool(u_pool, pool_w16, pool_scale):
    b, s, d = u_pool.shape
    cg = POOL_GROUP_DIM
    return pl.pallas_call(
        _pool_kernel,
        grid=(b, len(POOL_WINDOWS)),
        in_specs=[
            pl.BlockSpec((1, s, cg), lambda bi, g: (bi, 0, g)),
            pl.BlockSpec((1, cg, cg), lambda bi, g: (g, 0, 0)),
            pl.BlockSpec((1, cg), lambda bi, g: (0, g)),
        ],
        out_specs=pl.BlockSpec((1, s, cg), lambda bi, g: (bi, 0, g)),
        out_shape=jax.ShapeDtypeStruct((b, s, d), BF16),
        compiler_params=_params("parallel", "parallel"),
        name="pool_mixer",
    )(u_pool, pool_w16, pool_scale.reshape(1, d))


def _block_mask():
    r = lax.broadcasted_iota(jnp.int32, (RW_QUAD, RW_QUAD), 0)
    c = lax.broadcasted_iota(jnp.int32, (RW_QUAD, RW_QUAD), 1)
    return (r // R_HEAD_DIM) == (c // R_HEAD_DIM)


def _bd(x, mask):
    return jnp.where(mask, jnp.concatenate([x] * (RW_QUAD // R_HEAD_DIM), axis=0), 0.0)


def _seg_quad(x, ones_bd):
    hi, lo = _split(x)
    return _dot(hi, ones_bd) + _dot(lo, ones_bd)


def _seg_sum(x, ones_bd):
    parts = [_seg_quad(x[:, q * RW_QUAD:(q + 1) * RW_QUAD], ones_bd) for q in range(x.shape[1] // RW_QUAD)]
    return jnp.concatenate(parts, axis=1)


def _chunk_quad(r, k, v, a, b, lw, cum, mask):
    c = RW_CHUNK
    cum_end = cum[c - 1:c, :]
    at = a * jnp.exp(cum - lw)
    rt = r * jnp.exp(cum)
    e_neg = jnp.exp(-cum)
    bt = b * e_neg
    kt = k * e_neg
    e_end = jnp.exp(cum_end - cum)
    bh = b * e_end
    kh = k * e_end

    lhs = jnp.concatenate([at, rt], axis=0)
    rhs = jnp.concatenate([_bd(bt, mask), _bd(kt, mask)], axis=0)
    sc = _mm(lhs, rhs, "nt")
    ii = lax.broadcasted_iota(jnp.int32, (c, RW_QUAD), 0)
    jj = lax.broadcasted_iota(jnp.int32, (c, RW_QUAD), 1) % c
    strict = ii > jj
    incl = ii >= jj
    a_ab = jnp.where(strict, sc[:c, :RW_QUAD], 0.0)
    a_ak = jnp.where(strict, sc[:c, RW_QUAD:], 0.0)
    a_rb = jnp.where(incl, sc[c:, :RW_QUAD], 0.0)
    a_rk = jnp.where(incl, sc[c:, RW_QUAD:], 0.0)

    av = _mm(jnp.concatenate([a_ak, a_rk], axis=0), _bd(v, mask))
    xa = at
    xu = av[:c]
    pk = a_ab
    n_steps = int(math.log2(c))
    for step in range(n_steps):
        xa = xa + _mm(pk, _bd(xa, mask))
        xu = xu + _mm(pk, _bd(xu, mask))
        if step + 1 < n_steps:
            pk = _mm(pk, _bd(pk, mask))

    rp = rt + _mm(a_rb, _bd(xa, mask))
    y0 = av[c:] + _mm(a_rb, _bd(xu, mask))
    z1 = jnp.where(mask, _mm(bh, xa, "tn"), 0.0)
    z2 = jnp.where(mask, _mm(bh, xu, "tn") + _mm(kh, v, "tn"), 0.0)

    def fold(z):
        out = z[:c]
        for h in range(1, RW_QUAD // R_HEAD_DIM):
            out = out + z[h * c:(h + 1) * c]
        return out

    m = fold(z1) + jnp.where(ii == jj, jnp.exp(cum_end), 0.0)
    n = fold(z2)
    return rp, y0, m, n


def _rwkv_prep_kernel(rkv_ref, rkv_halo_ref, lora_ref, lora_halo_ref, mu_rkv_ref, mu_lora_ref, vec_ref,
                      w2_ref, a2_ref, g2_ref, rp_ref, y0_ref, m_ref, n_ref, bonus_ref, gate_ref):
    t = pl.program_id(1)
    ts = rkv_ref.shape[1]
    d = D_MODEL

    def shifted(x, halo, mu):
        row = lax.broadcasted_iota(jnp.int32, x.shape, 0)
        before = jnp.where(t > 0, halo[V7X_SUBLANES - 1:V7X_SUBLANES, :], 0.0)
        prev = jnp.where(row == 0, before, pltpu.roll(x, 1, 0))
        return x + (prev - x) * mu

    xs = shifted(rkv_ref[0], rkv_halo_ref[0], mu_rkv_ref[...])
    ls = shifted(lora_ref[0], lora_halo_ref[0], mu_lora_ref[...])
    r = xs[:, :d]
    k = xs[:, d:2 * d]
    v = xs[:, 2 * d:]
    vec = vec_ref[...]
    w0, a0, k_k, k_a, r_k = vec[0:1], vec[1:2], vec[2:3], vec[3:4], vec[4:5]

    z = w0 + _dot(jnp.tanh(ls[:, :128]).astype(BF16), w2_ref[...])
    lw = -math.exp(-0.5) * jax.nn.sigmoid(z)
    a_sig = jax.nn.sigmoid(a0 + _dot(ls[:, 128:256].astype(BF16), a2_ref[...]))
    gate_ref[0] = _dot(jax.nn.sigmoid(ls[:, 256:]).astype(BF16), g2_ref[...])

    mask = _block_mask()
    ones_bd = mask.astype(F32)
    kk = k * k_k
    kk = kk / jnp.maximum(jnp.sqrt(_seg_sum(kk * kk, ones_bd)), 1e-12)
    k2 = k * (1.0 + (a_sig - 1.0) * k_a)
    a_vec = -kk
    b_vec = kk * a_sig
    bonus_ref[0] = _seg_sum(r * k2 * r_k, ones_bd) * v

    row_in_chunk = lax.broadcasted_iota(jnp.int32, lw.shape, 0) % RW_CHUNK
    cum = lw
    shift = 1
    while shift < RW_CHUNK:
        cum = cum + jnp.where(row_in_chunk >= shift, pltpu.roll(cum, shift, 0), 0.0)
        shift *= 2

    for c in range(ts // RW_CHUNK):
        rows = slice(c * RW_CHUNK, (c + 1) * RW_CHUNK)
        for q in range(d // RW_QUAD):
            cols = slice(q * RW_QUAD, (q + 1) * RW_QUAD)
            rp, y0, m, n = _chunk_quad(r[rows, cols], k2[rows, cols], v[rows, cols], a_vec[rows, cols],
                                       b_vec[rows, cols], lw[rows, cols], cum[rows, cols], mask)
            rp_ref[0, rows, cols] = rp
            y0_ref[0, rows, cols] = y0
            m_ref[0, rows, cols] = m
            n_ref[0, rows, cols] = n


def _rwkv_prep(u_rkv, u_lora, mu_rkv, mu_lora, vecs, w2p, a2p, g2p):
    b, s, w3 = u_rkv.shape
    d = D_MODEL
    ts = _tile(s, RW_PREP_TILE)
    halo_blocks = ts // V7X_SUBLANES

    def tile_spec(width):
        return pl.BlockSpec((1, ts, width), lambda bi, t: (bi, t, 0))

    def halo_spec(width):
        return pl.BlockSpec((1, V7X_SUBLANES, width), lambda bi, t: (bi, jnp.maximum(t * halo_blocks - 1, 0), 0))

    def const_spec(shape):
        return pl.BlockSpec(shape, lambda bi, t: (0,) * len(shape))

    out = jax.ShapeDtypeStruct((b, s, d), F32)
    return pl.pallas_call(
        _rwkv_prep_kernel,
        grid=(b, s // ts),
        in_specs=[tile_spec(w3), halo_spec(w3), tile_spec(LORA_PAD), halo_spec(LORA_PAD),
                  const_spec((1, w3)), const_spec((1, LORA_PAD)), const_spec(vecs.shape),
                  const_spec(w2p.shape), const_spec(a2p.shape), const_spec(g2p.shape)],
        out_specs=[tile_spec(d)] * 6,
        out_shape=[out] * 6,
        compiler_params=_params("parallel", "parallel"),
        name="rwkv_prep",
    )(u_rkv, u_rkv, u_lora, u_lora, mu_rkv, mu_lora, vecs, w2p, a2p, g2p)


def _rwkv_scan_kernel(rp_ref, y0_ref, m_ref, n_ref, bonus_ref, gate_ref, vec_ref, o_ref, state_ref):
    @pl.when(pl.program_id(1) == 0)
    def _():
        state_ref[...] = jnp.zeros_like(state_ref)

    ts = rp_ref.shape[1]
    mask = _block_mask()
    inv_n = 1.0 / R_HEAD_DIM
    ones_bd = mask.astype(F32)
    vec = vec_ref[...]
    lnx_g, lnx_b = vec[5:6], vec[6:7]
    for q in range(D_MODEL // RW_QUAD):
        cols = slice(q * RW_QUAD, (q + 1) * RW_QUAD)
        state = state_ref[:, cols]
        for c in range(ts // RW_CHUNK):
            rows = slice(c * RW_CHUNK, (c + 1) * RW_CHUNK)
            lhs = jnp.concatenate([rp_ref[0, rows, cols], m_ref[0, rows, cols]], axis=0)
            out = _mm(lhs, _bd(state, mask), passes=RW_SCAN_PASSES)
            y = out[:RW_CHUNK] + y0_ref[0, rows, cols]
            state = out[RW_CHUNK:] + n_ref[0, rows, cols]
            mu = _seg_quad(y, ones_bd) * inv_n
            dy = y - mu
            var = _seg_quad(dy * dy, ones_bd) * inv_n
            yn = dy * lax.rsqrt(var + RWKV_GN_EPS) * lnx_g[:, cols] + lnx_b[:, cols]
            o_ref[0, rows, cols] = ((yn + bonus_ref[0, rows, cols]) * gate_ref[0, rows, cols]).astype(o_ref.dtype)
        state_ref[:, cols] = state


def _rwkv_scan(rp, y0, m, n, bonus, gate, vecs):
    b, s, d = rp.shape
    ts = _tile(s, RW_SCAN_TILE)
    tile_spec = pl.BlockSpec((1, ts, d), lambda bi, t: (bi, t, 0))
    return pl.pallas_call(
        _rwkv_scan_kernel,
        grid=(b, s // ts),
        in_specs=[tile_spec] * 6 + [pl.BlockSpec(vecs.shape, lambda bi, t: (0, 0))],
        out_specs=tile_spec,
        out_shape=jax.ShapeDtypeStruct((b, s, d), BF16),
        scratch_shapes=[pltpu.VMEM((R_HEAD_DIM, d), F32)],
        compiler_params=_params("parallel", "arbitrary"),
        name="rwkv_scan",
    )(rp, y0, m, n, bonus, gate, vecs)


def _merge_kernel(ya_ref, yb_ref, yc_ref, ug_ref, bg_ref, h_ref, wa_ref, wb_ref, wc_ref, wo_ref,
                  g_ref, b_ref, h32_ref, h16_ref):
    d = D_MODEL
    ug = ug_ref[...]
    bg = bg_ref[...]
    merged = jnp.zeros((ya_ref.shape[0], d), F32)
    for idx, (y_ref, w_ref) in enumerate(((ya_ref, wa_ref), (yb_ref, wb_ref), (yc_ref, wc_ref))):
        gate = jax.nn.sigmoid(ug[:, idx * d:(idx + 1) * d] + bg[idx:idx + 1, :])
        merged = merged + gate * _dot(y_ref[...], w_ref[...])
    z = DN_ALPHA * h_ref[...] + _dot(merged.astype(BF16), wo_ref[...])
    h = _ln(z, g_ref[...], b_ref[...], LN_EPS)
    h32_ref[...] = h
    h16_ref[...] = h.astype(BF16)


def _merge(ya, yb, yc, u_gate, b_gate, h32, wa, wb, wc, wo, ln_g, ln_b):
    t, d = h32.shape
    tm = _tile(t, FUSED_ROW_TILE)
    row = pl.BlockSpec((tm, d), lambda i: (i, 0))
    wspec = pl.BlockSpec((d, d), lambda i: (0, 0))
    vec = pl.BlockSpec((1, d), lambda i: (0, 0))
    return pl.pallas_call(
        _merge_kernel,
        grid=(t // tm,),
        in_specs=[row, row, row, pl.BlockSpec((tm, 3 * d), lambda i: (i, 0)),
                  pl.BlockSpec((8, d), lambda i: (0, 0)), row, wspec, wspec, wspec, wspec, vec, vec],
        out_specs=[row, row],
        out_shape=[jax.ShapeDtypeStruct((t, d), F32), jax.ShapeDtypeStruct((t, d), BF16)],
        compiler_params=_params("parallel"),
        name="merge_out",
    )(ya, yb, yc, u_gate, b_gate, h32, wa, wb, wc, wo, ln_g.reshape(1, d), ln_b.reshape(1, d))


def _cross_kernel(h16_ref, h32_ref, kv_ref, wq_ref, wo_ref, g_ref, b_ref, o32_ref, o16_ref):
    d = D_MODEL
    q = _dot(h16_ref[0], wq_ref[...]).astype(BF16)
    kv = kv_ref[0]
    heads = []
    for hd in range(X_HEADS):
        cols = slice(hd * X_HEAD_DIM, (hd + 1) * X_HEAD_DIM)
        s = _dot_nt(q[:, cols], kv[:, cols]) * (X_HEAD_DIM ** -0.5)
        p = jnp.exp(s - jnp.max(s, -1, keepdims=True))
        o = _dot(p.astype(BF16), kv[:, d + hd * X_HEAD_DIM:d + (hd + 1) * X_HEAD_DIM])
        heads.append((o / jnp.sum(p, -1, keepdims=True)).astype(BF16))
    attn = jnp.concatenate(heads, axis=1)
    z = DN_ALPHA * h32_ref[0] + _dot(attn, wo_ref[...])
    h = _ln(z, g_ref[...], b_ref[...], LN_EPS)
    o32_ref[0] = h
    o16_ref[0] = h.astype(BF16)


def _cross(h16, h32, kv, wq, wo, ln_g, ln_b):
    b, s, d = h32.shape
    tm = _tile(s, FUSED_ROW_TILE)
    row = pl.BlockSpec((1, tm, d), lambda bi, i: (bi, i, 0))
    wspec = pl.BlockSpec((d, d), lambda bi, i: (0, 0))
    vec = pl.BlockSpec((1, d), lambda bi, i: (0, 0))
    n_mem = kv.shape[1]
    return pl.pallas_call(
        _cross_kernel,
        grid=(b, s // tm),
        in_specs=[row, row, pl.BlockSpec((1, n_mem, 2 * d), lambda bi, i: (bi, 0, 0)), wspec, wspec, vec, vec],
        out_specs=[row, row],
        out_shape=[jax.ShapeDtypeStruct((b, s, d), F32), jax.ShapeDtypeStruct((b, s, d), BF16)],
        compiler_params=_params("parallel", "parallel"),
        name="cross_attn",
    )(h16, h32, kv, wq, wo, ln_g.reshape(1, d), ln_b.reshape(1, d))


def _mlp_kernel(h16_ref, h32_ref, w1_ref, w2_ref, g_ref, b_ref, o32_ref, o16_ref, acc_ref):
    f = pl.program_id(1)

    @pl.when(f == 0)
    def _():
        acc_ref[...] = jnp.zeros_like(acc_ref)

    a = jnp.maximum(_dot(h16_ref[...], w1_ref[...]), 0.0)
    acc_ref[...] += _dot((a * a).astype(BF16), w2_ref[...])

    @pl.when(f == pl.num_programs(1) - 1)
    def _():
        h = _ln(DN_ALPHA * h32_ref[...] + acc_ref[...], g_ref[...], b_ref[...], LN_EPS)
        o32_ref[...] = h
        o16_ref[...] = h.astype(BF16)


def _mlp(h16, h32, w1, w2, ln_g, ln_b):
    t, d = h32.shape
    dff = w1.shape[1]
    tm = _tile(t, MLP_ROW_TILE)
    tf = _tile(dff, MLP_FF_TILE)
    row = pl.BlockSpec((tm, d), lambda i, f: (i, 0))
    vec = pl.BlockSpec((1, d), lambda i, f: (0, 0))
    return pl.pallas_call(
        _mlp_kernel,
        grid=(t // tm, dff // tf),
        in_specs=[row, row, pl.BlockSpec((d, tf), lambda i, f: (0, f)), pl.BlockSpec((tf, d), lambda i, f: (f, 0)),
                  vec, vec],
        out_specs=[row, row],
        out_shape=[jax.ShapeDtypeStruct((t, d), F32), jax.ShapeDtypeStruct((t, d), BF16)],
        scratch_shapes=[pltpu.VMEM((tm, d), F32)],
        compiler_params=_params("parallel", "arbitrary"),
        name="mlp",
    )(h16, h32, w1, w2, ln_g.reshape(1, d), ln_b.reshape(1, d))


def _pad_rows(w, rows):
    return jnp.pad(w, ((0, rows - w.shape[0]), (0, 0)))


def _lora_layout(w):
    pad = lambda t, n: jnp.pad(t, [(0, 0)] * (t.ndim - 1) + [(0, n - t.shape[-1])])
    xw = w[..., :DECAY_LORA]
    xa = w[..., DECAY_LORA:DECAY_LORA + AAA_LORA]
    xg = w[..., DECAY_LORA + AAA_LORA:]
    return jnp.concatenate([pad(xw, 128), pad(xa, 128), pad(xg, 256)], axis=-1)


def kernel(x, mem, positions, ln_in_g, ln_in_b, w_in, b_gate, lam_q1, lam_k1, lam_q2, lam_k2, attn_subln_g, w_br_attn, pool_w, pool_scale, w_br_pool, rwkv_mu, rwkv_w0, rwkv_w2, rwkv_a0, rwkv_a2, rwkv_g2, rwkv_k_k, rwkv_k_a, rwkv_r_k, rwkv_lnx_g, rwkv_lnx_b, w_br_rwkv, w_out, ln1_g, ln1_b, w_xq, w_xkv, w_xo, ln2_g, ln2_b, w_ff1, w_ff2, ln3_g, ln3_b):
    bsz, seq, d = x.shape
    t = bsz * seq
    n_mem = mem.shape[1]
    depth = w_in.shape[0]

    inv_freq = 1.0 / (ROPE_THETA ** (jnp.arange(0, A_HEAD_DIM, 2, dtype=F32) / A_HEAD_DIM))
    ang = positions.astype(F32).reshape(t, 1) * inv_freq
    cos, sin = jnp.cos(ang), jnp.sin(ang)
    cos_t = jnp.concatenate([cos, cos, cos, cos], axis=-1)
    sin_t = jnp.concatenate([-sin, sin, -sin, sin], axis=-1)

    c_qk = 2 * A_HEADS * 2 * A_HEAD_DIM
    c_v = c_qk + A_HEADS * A_VAL_DIM
    c_pool = c_v + d
    c_rkv = c_pool + 3 * d
    c_lora = c_rkv + DECAY_LORA + AAA_LORA + GATE_LORA

    h32, h16 = _ln_in(x.reshape(t, d), ln_in_g, ln_in_b)
    mem2d = mem.reshape(bsz * n_mem, d)

    for l in range(depth):
        w = w_in[l]
        w_qk = w[:, :c_qk].astype(BF16)
        w_v = w[:, c_qk:c_v].astype(BF16)
        w_pool = w[:, c_v:c_pool].astype(BF16)
        w_rkv = w[:, c_pool:c_rkv].astype(BF16)
        w_lora = _lora_layout(w[:, c_rkv:c_lora]).astype(BF16)
        w_gate = w[:, c_lora:].astype(BF16)

        qk = _proj(h16, w_qk, BF16, rope=(cos_t, sin_t), name="proj_qk").reshape(bsz, seq, c_qk)
        v = _proj(h16, w_v, BF16, name="proj_v").reshape(bsz, seq, d)
        u_pool = _proj(h16, w_pool, F32, name="proj_pool").reshape(bsz, seq, d)
        u_rkv = _proj(h16, w_rkv, F32, name="proj_rkv").reshape(bsz, seq, 3 * d)
        u_lora = _proj(h16, w_lora, F32, name="proj_lora").reshape(bsz, seq, LORA_PAD)
        u_gate = _proj(h16, w_gate, F32, name="proj_gate")

        lam_init = 0.8 - 0.6 * math.exp(-0.3 * l)
        lam_rows = jnp.stack([lam_q1[l], lam_k1[l], lam_q2[l], lam_k2[l]]).astype(F32)
        y_a = _diff_attn(qk, v, lam_rows, attn_subln_g[l], lam_init)

        y_b = _pool(u_pool, pool_w[l].astype(BF16), pool_scale[l])

        mu = rwkv_mu[l]
        mu_rkv = mu[:3 * d].reshape(1, 3 * d)
        mu_lora = _lora_layout(mu[3 * d:]).reshape(1, LORA_PAD)
        vecs = jnp.stack([rwkv_w0[l], rwkv_a0[l], rwkv_k_k[l], rwkv_k_a[l], rwkv_r_k[l].reshape(d),
                          rwkv_lnx_g[l], rwkv_lnx_b[l], jnp.zeros((d,), F32)])
        w2p = _pad_rows(rwkv_w2[l], 128).astype(BF16)
        a2p = _pad_rows(rwkv_a2[l], 128).astype(BF16)
        g2p = _pad_rows(rwkv_g2[l], 256).astype(BF16)
        rp, y0, m_c, n_c, bonus, gate = _rwkv_prep(u_rkv, u_lora, mu_rkv, mu_lora, vecs, w2p, a2p, g2p)
        y_c = _rwkv_scan(rp, y0, m_c, n_c, bonus, gate, vecs)

        bg = jnp.pad(b_gate[l], ((0, 8 - b_gate.shape[1]), (0, 0)))
        h32, h16 = _merge(y_a.reshape(t, d), y_b.reshape(t, d), y_c.reshape(t, d), u_gate, bg, h32,
                          w_br_attn[l].astype(BF16), w_br_pool[l].astype(BF16), w_br_rwkv[l].astype(BF16),
                          w_out[l].astype(BF16), ln1_g[l], ln1_b[l])

        kv = _proj(mem2d, w_xkv[l].astype(BF16), BF16, name="proj_kv").reshape(bsz, n_mem, 2 * d)
        h32, h16 = _cross(h16.reshape(bsz, seq, d), h32.reshape(bsz, seq, d), kv, w_xq[l].astype(BF16),
                          w_xo[l].astype(BF16), ln2_g[l], ln2_b[l])
        h32, h16 = h32.reshape(t, d), h16.reshape(t, d)

        h32, h16 = _mlp(h16, h32, w_ff1[l].astype(BF16), w_ff2[l].astype(BF16), ln3_g[l], ln3_b[l])

    return h32.reshape(bsz, seq, d)
```

```python
import functools
import math

import jax
import jax.numpy as jnp
from jax import lax
from jax.experimental import pallas as pl
from jax.experimental.pallas import tpu as pltpu

F32 = jnp.float32
BF16 = jnp.bfloat16

D_MODEL = 1024
DEPTH = 2
CHUNK = 64
A_HEADS = 8
A_HEAD_DIM = 64
A_VAL_DIM = 2 * A_HEAD_DIM
ROPE_THETA = 10000.0
POOL_WINDOWS = (2, 4, 8, 16)
POOL_GROUP_DIM = D_MODEL // len(POOL_WINDOWS)
R_HEAD_DIM = 64
DECAY_LORA = 64
AAA_LORA = 64
GATE_LORA = 160
X_HEADS = 4
X_HEAD_DIM = D_MODEL // X_HEADS
D_FF = 4 * D_MODEL
DN_ALPHA = (2 * DEPTH) ** 0.25
LN_EPS = 1e-5
RMS_EPS = 1e-5
RWKV_GN_EPS = 64e-5
NEG_INF = -1e30

V7X_LANES = 128
V7X_SUBLANES = 8
V7X_MXU_DIM = 256
V7X_VMEM_LIMIT_BYTES = 56 * 1024 * 1024

ROW_TILE = 1024
COL_TILE = 512
FUSED_ROW_TILE = 256
MLP_ROW_TILE = 512
MLP_FF_TILE = 1024
ATTN_Q_TILE = 256
ATTN_K_TILE = 256
RW_CHUNK = 64
RW_QUAD = V7X_MXU_DIM
RW_TILE = 128
HALO_ROWS = 16
LORA_PAD = 512
HIGHEST = lax.Precision.HIGHEST
RW_PASSES = 1
RW_SCAN_PASSES = 1


def _tile(n, pref):
    return pref if n % pref == 0 else n


def _params(*sem):
    return pltpu.CompilerParams(dimension_semantics=sem, vmem_limit_bytes=V7X_VMEM_LIMIT_BYTES)


def _ln(z, g, b, eps):
    mu = jnp.mean(z, -1, keepdims=True)
    d = z - mu
    var = jnp.mean(d * d, -1, keepdims=True)
    return d * lax.rsqrt(var + eps) * g + b


def _dot(a, b, precision=None):
    return jnp.dot(a, b, preferred_element_type=F32, precision=precision)


def _dot_nt(a, b, precision=None):
    return lax.dot_general(a, b, (((1,), (1,)), ((), ())), preferred_element_type=F32, precision=precision)


def _dot_tn(a, b, precision=None):
    return lax.dot_general(a, b, (((0,), (0,)), ((), ())), preferred_element_type=F32, precision=precision)


_DIMS = {"nn": (((1,), (0,)), ((), ())), "nt": (((1,), (1,)), ((), ())), "tn": (((0,), (0,)), ((), ()))}


def _split(x):
    hi = x.astype(BF16)
    return hi, (x - hi.astype(F32)).astype(BF16)


def _mm(a, b, kind="nn", passes=None):
    passes = RW_PASSES if passes is None else passes
    dg = functools.partial(lax.dot_general, dimension_numbers=_DIMS[kind], preferred_element_type=F32)
    if passes == 6:
        return dg(a, b, precision=HIGHEST)
    if passes == 1:
        return dg(a.astype(BF16), b.astype(BF16))
    a_hi, a_lo = _split(a)
    b_hi, b_lo = _split(b)
    return dg(a_hi, b_hi) + (dg(a_hi, b_lo) + dg(a_lo, b_hi))


def _ln_in_kernel(x_ref, g_ref, b_ref, h32_ref, h16_ref):
    h = _ln(x_ref[...], g_ref[...], b_ref[...], LN_EPS)
    h32_ref[...] = h
    h16_ref[...] = h.astype(BF16)


def _ln_in(x2d, g, b):
    t, d = x2d.shape
    tm = _tile(t, ROW_TILE)
    row = pl.BlockSpec((tm, d), lambda i: (i, 0))
    vec = pl.BlockSpec((1, d), lambda i: (0, 0))
    return pl.pallas_call(
        _ln_in_kernel,
        grid=(t // tm,),
        in_specs=[row, vec, vec],
        out_specs=[row, row],
        out_shape=[jax.ShapeDtypeStruct((t, d), F32), jax.ShapeDtypeStruct((t, d), BF16)],
        compiler_params=_params("parallel"),
        name="ln_in",
    )(x2d, g.reshape(1, d), b.reshape(1, d))


def _proj_kernel(x_ref, w_ref, o_ref):
    o_ref[...] = _dot(x_ref[...].astype(BF16), w_ref[...]).astype(o_ref.dtype)


def _proj_rope_kernel(x_ref, w_ref, cos_ref, sin_ref, o_ref):
    t = _dot(x_ref[...], w_ref[...])
    cos = cos_ref[...]
    sin = sin_ref[...]
    lane = lax.broadcasted_iota(jnp.int32, cos.shape, 1)
    first_half = (lane & (A_HEAD_DIM - 1)) < A_HEAD_DIM // 2
    for j in range(t.shape[1] // V7X_LANES):
        blk = t[:, j * V7X_LANES:(j + 1) * V7X_LANES]
        partner = jnp.where(first_half,
                            pltpu.roll(blk, V7X_LANES - A_HEAD_DIM // 2, 1),
                            pltpu.roll(blk, A_HEAD_DIM // 2, 1))
        o_ref[:, j * V7X_LANES:(j + 1) * V7X_LANES] = (blk * cos + partner * sin).astype(o_ref.dtype)


def _proj_gate_kernel(x_ref, w_ref, b_ref, o_ref):
    o_ref[...] = jax.nn.sigmoid(_dot(x_ref[...], w_ref[...]) + b_ref[...]).astype(o_ref.dtype)


def _proj(x, w16, out_dtype, rope=None, gate_bias=None, name="proj"):
    m, k = x.shape
    n = w16.shape[1]
    tm = _tile(m, ROW_TILE)
    tn = _tile(n, COL_TILE)
    in_specs = [pl.BlockSpec((tm, k), lambda i, j: (i, 0)), pl.BlockSpec((k, tn), lambda i, j: (0, j))]
    args = [x, w16]
    body = _proj_kernel
    if rope is not None:
        in_specs += [pl.BlockSpec((tm, V7X_LANES), lambda i, j: (i, 0))] * 2
        args += list(rope)
        body = _proj_rope_kernel
    if gate_bias is not None:
        in_specs += [pl.BlockSpec((1, tn), lambda i, j: (0, j))]
        args += [gate_bias]
        body = _proj_gate_kernel
    return pl.pallas_call(
        body,
        grid=(m // tm, n // tn),
        in_specs=in_specs,
        out_specs=pl.BlockSpec((tm, tn), lambda i, j: (i, j)),
        out_shape=jax.ShapeDtypeStruct((m, n), out_dtype),
        compiler_params=_params("parallel", "parallel"),
        name=name,
    )(*args)


def _diff_attn_kernel(lam_ref, g_ref, q_ref, k_ref, v_ref, o_ref, s_buf, p_buf, *, lam_init):
    i = pl.program_id(2)
    tq = q_ref.shape[1]
    tk = min(ATTN_K_TILE, k_ref.shape[1])
    q = q_ref[0] * jnp.asarray(A_HEAD_DIM ** -0.5, BF16)
    lane = lax.broadcasted_iota(jnp.int32, q.shape, 1)
    zero = jnp.zeros_like(q)
    qs = jnp.concatenate([jnp.where(lane < A_HEAD_DIM, q, zero), jnp.where(lane >= A_HEAD_DIM, q, zero)], axis=0)

    def keys(j):
        return k_ref[0, pl.ds(pl.multiple_of(j * tk, tk), tk), :]

    def values(j):
        return v_ref[0, pl.ds(pl.multiple_of(j * tk, tk), tk), :]

    def prev_values_product(j):
        return _dot_tn(values(jnp.maximum(j - 1, 0)), p_buf[1 - (j & 1)])

    def softmax_step(j, carry, s, pv, masked):
        m, l, acc = carry
        if masked:
            kpos = j * tk + lax.broadcasted_iota(jnp.int32, (tk, 2 * tq), 0)
            qpos = i * tq + (lax.broadcasted_iota(jnp.int32, (tk, 2 * tq), 1) & (tq - 1))
            chunk_shift = CHUNK.bit_length() - 1
            s = jnp.where((kpos >> chunk_shift) <= (qpos >> chunk_shift), s, NEG_INF)
        m_new = jnp.maximum(m, jnp.max(s, axis=0, keepdims=True))
        alpha = jnp.exp(m - m_new)
        p = jnp.exp(s - m_new)
        l = alpha * l + jnp.sum(p.reshape(tk // V7X_SUBLANES, V7X_SUBLANES, 2 * tq), axis=0)
        return (m_new, l, alpha * (acc + pv)), p.astype(BF16)

    def body(j, carry):
        s = s_buf[j & 1]
        pv = prev_values_product(j)
        s_next = _dot_nt(keys(j + 1), qs)
        carry, p = softmax_step(j, carry, s, pv, False)
        s_buf[1 - (j & 1)] = s_next
        p_buf[j & 1] = p
        return carry

    s_buf[0] = _dot_nt(keys(0), qs)
    p_buf[1] = jnp.zeros(p_buf.shape[1:], BF16)
    init = (jnp.full((1, 2 * tq), NEG_INF, F32), jnp.zeros((V7X_SUBLANES, 2 * tq), F32),
            jnp.zeros((A_VAL_DIM, 2 * tq), F32))
    n_full = (i * tq) // tk
    carry = lax.fori_loop(0, n_full, body, init)
    (_, l, acc), p = softmax_step(n_full, carry, s_buf[n_full & 1], prev_values_product(n_full), True)
    acc = acc + _dot_tn(values(n_full), p)
    out = acc / jnp.sum(l, axis=0, keepdims=True)

    lam_rows = lam_ref[...]
    lam = (jnp.exp(jnp.sum(lam_rows[0:1] * lam_rows[1:2], -1, keepdims=True))
           - jnp.exp(jnp.sum(lam_rows[2:3] * lam_rows[3:4], -1, keepdims=True)) + lam_init)
    o = out[:, :tq] - lam * out[:, tq:]
    o = o * lax.rsqrt(jnp.mean(o * o, axis=0, keepdims=True) + RMS_EPS) * g_ref[...]
    o_ref[0] = (o * (1.0 - lam_init)).T.astype(o_ref.dtype)


def _diff_attn(qk, v, lam_rows, subln_g, lam_init):
    b, s, _ = v.shape
    tq = _tile(s, ATTN_Q_TILE)
    tk = min(ATTN_K_TILE, s)
    k_col0 = A_HEADS
    return pl.pallas_call(
        functools.partial(_diff_attn_kernel, lam_init=lam_init),
        grid=(b, A_HEADS, s // tq),
        in_specs=[
            pl.BlockSpec((4, A_HEAD_DIM), lambda bi, h, i: (0, 0)),
            pl.BlockSpec((A_VAL_DIM, 1), lambda bi, h, i: (0, 0)),
            pl.BlockSpec((1, tq, A_VAL_DIM), lambda bi, h, i: (bi, i, h)),
            pl.BlockSpec((1, s, A_VAL_DIM), lambda bi, h, i: (bi, 0, k_col0 + h)),
            pl.BlockSpec((1, s, A_VAL_DIM), lambda bi, h, i: (bi, 0, h)),
        ],
        out_specs=pl.BlockSpec((1, tq, A_VAL_DIM), lambda bi, h, i: (bi, i, h)),
        out_shape=jax.ShapeDtypeStruct(v.shape, BF16),
        scratch_shapes=[pltpu.VMEM((2, tk, 2 * tq), F32), pltpu.VMEM((2, tk, 2 * tq), BF16)],
        compiler_params=_params("parallel", "parallel", "arbitrary"),
        name="diff_attn",
    )(lam_rows, subln_g.reshape(A_VAL_DIM, 1), qk, qk, v)


def _pool_kernel(p_ref, w_ref, scale_ref, o_ref):
    g = pl.program_id(1)
    x = p_ref[0].astype(F32)
    row = lax.broadcasted_iota(jnp.int32, x.shape, 0)
    for gi, window in enumerate(POOL_WINDOWS):
        @pl.when(g == gi)
        def _(window=window):
            ws = x
            shift = 1
            while shift < window:
                ws = ws + jnp.where(row >= shift, pltpu.roll(ws, shift, 0), 0.0)
                shift *= 2
            count = jnp.minimum(row + 1, window).astype(F32)
            d = ws / count - x
            y = _dot(d.astype(BF16), w_ref[0]) * scale_ref[...]
            o_ref[0] = y.astype(o_ref.dtype)


def _pool(u_pool, pool_w16, pool_scale):
    b, s, d = u_pool.shape
    cg = POOL_GROUP_DIM
    return pl.pallas_call(
        _pool_kernel,
        grid=(b, len(POOL_WINDOWS)),
        in_specs=[
            pl.BlockSpec((1, s, cg), lambda bi, g: (bi, 0, g)),
            pl.BlockSpec((1, cg, cg), lambda bi, g: (g, 0, 0)),
            pl.BlockSpec((1, cg), lambda bi, g: (0, g)),
        ],
        out_specs=pl.BlockSpec((1, s, cg), lambda bi, g: (bi, 0, g)),
        out_shape=jax.ShapeDtypeStruct((b, s, d), BF16),
        compiler_params=_params("parallel", "parallel"),
        name="pool_mixer",
    )(u_pool, pool_w16, pool_scale.reshape(1, d))


def _block_mask():
    r = lax.broadcasted_iota(jnp.int32, (RW_QUAD, RW_QUAD), 0)
    c = lax.broadcasted_iota(jnp.int32, (RW_QUAD, RW_QUAD), 1)
    return (r // R_HEAD_DIM) == (c // R_HEAD_DIM)


def _bd(x, mask):
    tiled = jnp.concatenate([x] * (RW_QUAD // R_HEAD_DIM), axis=0)
    return jnp.where(mask, tiled, jnp.zeros_like(tiled))


def _seg_quad(x, ones_bd):
    hi, lo = _split(x)
    return _dot(hi, ones_bd) + _dot(lo, ones_bd)


def _seg_sum(x, ones_bd):
    parts = [_seg_quad(x[:, q * RW_QUAD:(q + 1) * RW_QUAD], ones_bd) for q in range(x.shape[1] // RW_QUAD)]
    return jnp.concatenate(parts, axis=1)


def _chunk_summaries(units, mask):
    c = RW_CHUNK
    ii = lax.broadcasted_iota(jnp.int32, (c, RW_QUAD), 0)
    jj = lax.broadcasted_iota(jnp.int32, (c, RW_QUAD), 1) % c
    strict = ii > jj
    incl = ii >= jj
    eye = (ii == jj).astype(F32)
    bd = lambda x: _bd(x, mask)

    scaled = []
    for r, k, v, a, b, lw, cum in units:
        cum_end = cum[c - 1:c, :]
        at = a * jnp.exp(cum - lw)
        rt = r * jnp.exp(cum)
        e_neg = jnp.exp(-cum)
        e_end = jnp.exp(cum_end - cum)
        scaled.append((at, rt, b * e_neg, k * e_neg, b * e_end, k * e_end, v, jnp.exp(cum_end)))

    scores = [_mm(jnp.concatenate([at, rt], axis=0), jnp.concatenate([bd(bt), bd(kt)], axis=0), "nt")
              for at, rt, bt, kt, _, _, _, _ in scaled]
    a_ab = [jnp.where(strict, sc[:c, :RW_QUAD], 0.0) for sc in scores]
    a_rb = [jnp.where(incl, sc[c:, :RW_QUAD], 0.0) for sc in scores]
    av = [_mm(jnp.concatenate([jnp.where(strict, sc[:c, RW_QUAD:], 0.0), jnp.where(incl, sc[c:, RW_QUAD:], 0.0)],
                              axis=0), bd(u[6])) for sc, u in zip(scores, scaled)]

    tinv = [a + eye for a in a_ab]
    pk = a_ab
    for _ in range(int(math.log2(c)) - 1):
        pk = [_mm(p, bd(p)) for p in pk]
        tinv = [t + _mm(p, bd(t)) for t, p in zip(tinv, pk)]

    out = []
    for t, arb, avu, (at, rt, _, _, bh, kh, v, w_end) in zip(tinv, a_rb, av, scaled):
        xa = _mm(t, bd(at))
        xu = _mm(t, bd(avu[:c]))
        rp = rt + _mm(arb, bd(xa))
        y0 = avu[c:] + _mm(arb, bd(xu))
        z1 = jnp.where(mask, _mm(bh, xa, "tn"), 0.0)
        z2 = jnp.where(mask, _mm(bh, xu, "tn") + _mm(kh, v, "tn"), 0.0)
        fold = lambda z: functools.reduce(jnp.add, [z[h * c:(h + 1) * c] for h in range(RW_QUAD // R_HEAD_DIM)])
        out.append((rp, y0, fold(z1) + eye * w_end, fold(z2)))
    return out


def _rwkv_kernel(rkv_ref, rkv_halo_ref, lora_ref, lora_halo_ref, mu_rkv_ref, mu_lora_ref, vec_ref,
                 w2_ref, a2_ref, g2_ref, o_ref, state_ref):
    t = pl.program_id(1)
    ts = rkv_ref.shape[1]
    d = D_MODEL

    @pl.when(t == 0)
    def _():
        state_ref[...] = jnp.zeros_like(state_ref)

    def shifted(x_ref, halo_ref, mu):
        x = x_ref[0].astype(F32)
        halo = halo_ref[0].astype(F32)
        row = lax.broadcasted_iota(jnp.int32, x.shape, 0)
        before = jnp.where(t > 0, halo[HALO_ROWS - 1:HALO_ROWS, :], 0.0)
        prev = jnp.where(row == 0, before, pltpu.roll(x, 1, 0))
        return x + (prev - x) * mu

    xs = shifted(rkv_ref, rkv_halo_ref, mu_rkv_ref[...])
    ls = shifted(lora_ref, lora_halo_ref, mu_lora_ref[...])
    r = xs[:, :d]
    k = xs[:, d:2 * d]
    v = xs[:, 2 * d:]
    vec = vec_ref[...]
    w0, a0, k_k, k_a, r_k = vec[0:1], vec[1:2], vec[2:3], vec[3:4], vec[4:5]

    z = w0 + _dot(jnp.tanh(ls[:, :128]).astype(BF16), w2_ref[...])
    lw = -math.exp(-0.5) * jax.nn.sigmoid(z)
    a_sig = jax.nn.sigmoid(a0 + _dot(ls[:, 128:256].astype(BF16), a2_ref[...]))
    gate = _dot(jax.nn.sigmoid(ls[:, 256:]).astype(BF16), g2_ref[...])

    mask = _block_mask()
    ones_bd = mask.astype(BF16)
    kk = k * k_k
    kk = kk / jnp.maximum(jnp.sqrt(_seg_sum(kk * kk, ones_bd)), 1e-12)
    k2 = k * (1.0 + (a_sig - 1.0) * k_a)
    a_vec = -kk
    b_vec = kk * a_sig
    bonus = _seg_sum(r * k2 * r_k, ones_bd) * v

    row_in_chunk = lax.broadcasted_iota(jnp.int32, lw.shape, 0) % RW_CHUNK
    cum = lw
    shift = 1
    while shift < RW_CHUNK:
        cum = cum + jnp.where(row_in_chunk >= shift, pltpu.roll(cum, shift, 0), 0.0)
        shift *= 2

    windows = [(slice(c * RW_CHUNK, (c + 1) * RW_CHUNK), slice(q * RW_QUAD, (q + 1) * RW_QUAD))
               for c in range(ts // RW_CHUNK) for q in range(d // RW_QUAD)]
    units = [tuple(x[rows, cols] for x in (r, k2, v, a_vec, b_vec, lw, cum)) for rows, cols in windows]
    summaries = _chunk_summaries(units, mask)

    inv_n = 1.0 / R_HEAD_DIM
    lnx_g, lnx_b = vec[5:6], vec[6:7]
    n_quads = d // RW_QUAD
    states = [state_ref[:, q * RW_QUAD:(q + 1) * RW_QUAD] for q in range(n_quads)]
    for (rows, cols), (rp, y0, m, n) in zip(windows, summaries):
        q = cols.start // RW_QUAD
        out = _mm(jnp.concatenate([rp, m], axis=0), _bd(states[q], mask), passes=RW_SCAN_PASSES)
        y = out[:RW_CHUNK] + y0
        states[q] = out[RW_CHUNK:] + n
        mu = _seg_quad(y, ones_bd) * inv_n
        dy = y - mu
        var = _seg_quad(dy * dy, ones_bd) * inv_n
        yn = dy * lax.rsqrt(var + RWKV_GN_EPS) * lnx_g[:, cols] + lnx_b[:, cols]
        o_ref[0, rows, cols] = ((yn + bonus[rows, cols]) * gate[rows, cols]).astype(o_ref.dtype)
    for q in range(n_quads):
        state_ref[:, q * RW_QUAD:(q + 1) * RW_QUAD] = states[q]


def _rwkv(u_rkv, u_lora, mu_rkv, mu_lora, vecs, w2p, a2p, g2p):
    b, s, w3 = u_rkv.shape
    d = D_MODEL
    ts = _tile(s, RW_TILE)
    halo_blocks = ts // HALO_ROWS

    def tile_spec(width):
        return pl.BlockSpec((1, ts, width), lambda bi, t: (bi, t, 0))

    def halo_spec(width):
        return pl.BlockSpec((1, HALO_ROWS, width), lambda bi, t: (bi, jnp.maximum(t * halo_blocks - 1, 0), 0))

    def const_spec(shape):
        return pl.BlockSpec(shape, lambda bi, t: (0,) * len(shape))

    return pl.pallas_call(
        _rwkv_kernel,
        grid=(b, s // ts),
        in_specs=[tile_spec(w3), halo_spec(w3), tile_spec(LORA_PAD), halo_spec(LORA_PAD),
                  const_spec((1, w3)), const_spec((1, LORA_PAD)), const_spec(vecs.shape),
                  const_spec(w2p.shape), const_spec(a2p.shape), const_spec(g2p.shape)],
        out_specs=tile_spec(d),
        out_shape=jax.ShapeDtypeStruct((b, s, d), BF16),
        scratch_shapes=[pltpu.VMEM((R_HEAD_DIM, d), F32)],
        compiler_params=_params("parallel", "arbitrary"),
        name="rwkv7",
    )(u_rkv, u_rkv, u_lora, u_lora, mu_rkv, mu_lora, vecs, w2p, a2p, g2p)


def _merge_kernel(ya_ref, yb_ref, yc_ref, gates_ref, h_ref, wa_ref, wb_ref, wc_ref, wo_ref,
                  g_ref, b_ref, h32_ref, h16_ref):
    d = D_MODEL
    merged = jnp.zeros((ya_ref.shape[0], d), F32)
    for idx, (y_ref, w_ref) in enumerate(((ya_ref, wa_ref), (yb_ref, wb_ref), (yc_ref, wc_ref))):
        gate = gates_ref[:, idx * d:(idx + 1) * d].astype(F32)
        merged = merged + gate * _dot(y_ref[...], w_ref[...])
    z = DN_ALPHA * h_ref[...] + _dot(merged.astype(BF16), wo_ref[...])
    h = _ln(z, g_ref[...], b_ref[...], LN_EPS)
    h32_ref[...] = h
    h16_ref[...] = h.astype(BF16)


def _merge(ya, yb, yc, gates, h32, wa, wb, wc, wo, ln_g, ln_b):
    t, d = h32.shape
    tm = _tile(t, FUSED_ROW_TILE)
    row = pl.BlockSpec((tm, d), lambda i: (i, 0))
    wspec = pl.BlockSpec((d, d), lambda i: (0, 0))
    vec = pl.BlockSpec((1, d), lambda i: (0, 0))
    return pl.pallas_call(
        _merge_kernel,
        grid=(t // tm,),
        in_specs=[row, row, row, pl.BlockSpec((tm, 3 * d), lambda i: (i, 0)),
                  row, wspec, wspec, wspec, wspec, vec, vec],
        out_specs=[row, row],
        out_shape=[jax.ShapeDtypeStruct((t, d), F32), jax.ShapeDtypeStruct((t, d), BF16)],
        compiler_params=_params("parallel"),
        name="merge_out",
    )(ya, yb, yc, gates, h32, wa, wb, wc, wo, ln_g.reshape(1, d), ln_b.reshape(1, d))


def _cross_kernel(h16_ref, h32_ref, kv_ref, wq_ref, wo_ref, g_ref, b_ref, o32_ref, o16_ref):
    d = D_MODEL
    q = _dot(h16_ref[0], wq_ref[...]).astype(BF16)
    kv = kv_ref[0]
    heads = []
    for hd in range(X_HEADS):
        cols = slice(hd * X_HEAD_DIM, (hd + 1) * X_HEAD_DIM)
        s = _dot_nt(q[:, cols], kv[:, cols]) * (X_HEAD_DIM ** -0.5)
        p = jnp.exp(s - jnp.max(s, -1, keepdims=True))
        o = _dot(p.astype(BF16), kv[:, d + hd * X_HEAD_DIM:d + (hd + 1) * X_HEAD_DIM])
        heads.append((o / jnp.sum(p, -1, keepdims=True)).astype(BF16))
    attn = jnp.concatenate(heads, axis=1)
    z = DN_ALPHA * h32_ref[0] + _dot(attn, wo_ref[...])
    h = _ln(z, g_ref[...], b_ref[...], LN_EPS)
    o32_ref[0] = h
    o16_ref[0] = h.astype(BF16)


def _cross(h16, h32, kv, wq, wo, ln_g, ln_b):
    b, s, d = h32.shape
    tm = _tile(s, FUSED_ROW_TILE)
    row = pl.BlockSpec((1, tm, d), lambda bi, i: (bi, i, 0))
    wspec = pl.BlockSpec((d, d), lambda bi, i: (0, 0))
    vec = pl.BlockSpec((1, d), lambda bi, i: (0, 0))
    n_mem = kv.shape[1]
    return pl.pallas_call(
        _cross_kernel,
        grid=(b, s // tm),
        in_specs=[row, row, pl.BlockSpec((1, n_mem, 2 * d), lambda bi, i: (bi, 0, 0)), wspec, wspec, vec, vec],
        out_specs=[row, row],
        out_shape=[jax.ShapeDtypeStruct((b, s, d), F32), jax.ShapeDtypeStruct((b, s, d), BF16)],
        compiler_params=_params("parallel", "parallel"),
        name="cross_attn",
    )(h16, h32, kv, wq, wo, ln_g.reshape(1, d), ln_b.reshape(1, d))


def _mlp_kernel(h16_ref, h32_ref, w1_ref, w2_ref, g_ref, b_ref, o32_ref, o16_ref, acc_ref):
    f = pl.program_id(1)

    @pl.when(f == 0)
    def _():
        acc_ref[...] = jnp.zeros_like(acc_ref)

    a = jnp.maximum(_dot(h16_ref[...], w1_ref[...]), 0.0)
    acc_ref[...] += _dot((a * a).astype(BF16), w2_ref[...])

    @pl.when(f == pl.num_programs(1) - 1)
    def _():
        h = _ln(DN_ALPHA * h32_ref[...] + acc_ref[...], g_ref[...], b_ref[...], LN_EPS)
        o32_ref[...] = h
        o16_ref[...] = h.astype(BF16)


def _mlp(h16, h32, w1, w2, ln_g, ln_b):
    t, d = h32.shape
    dff = w1.shape[1]
    tm = _tile(t, MLP_ROW_TILE)
    tf = _tile(dff, MLP_FF_TILE)
    row = pl.BlockSpec((tm, d), lambda i, f: (i, 0))
    vec = pl.BlockSpec((1, d), lambda i, f: (0, 0))
    return pl.pallas_call(
        _mlp_kernel,
        grid=(t // tm, dff // tf),
        in_specs=[row, row, pl.BlockSpec((d, tf), lambda i, f: (0, f)), pl.BlockSpec((tf, d), lambda i, f: (f, 0)),
                  vec, vec],
        out_specs=[row, row],
        out_shape=[jax.ShapeDtypeStruct((t, d), F32), jax.ShapeDtypeStruct((t, d), BF16)],
        scratch_shapes=[pltpu.VMEM((tm, d), F32)],
        compiler_params=_params("parallel", "arbitrary"),
        name="mlp",
    )(h16, h32, w1, w2, ln_g.reshape(1, d), ln_b.reshape(1, d))


def _pad_rows(w, rows):
    return jnp.pad(w, ((0, rows - w.shape[0]), (0, 0)))


def _lora_layout(w):
    pad = lambda t, n: jnp.pad(t, [(0, 0)] * (t.ndim - 1) + [(0, n - t.shape[-1])])
    xw = w[..., :DECAY_LORA]
    xa = w[..., DECAY_LORA:DECAY_LORA + AAA_LORA]
    xg = w[..., DECAY_LORA + AAA_LORA:]
    return jnp.concatenate([pad(xw, 128), pad(xa, 128), pad(xg, 256)], axis=-1)


def kernel(x, mem, positions, ln_in_g, ln_in_b, w_in, b_gate, lam_q1, lam_k1, lam_q2, lam_k2, attn_subln_g, w_br_attn, pool_w, pool_scale, w_br_pool, rwkv_mu, rwkv_w0, rwkv_w2, rwkv_a0, rwkv_a2, rwkv_g2, rwkv_k_k, rwkv_k_a, rwkv_r_k, rwkv_lnx_g, rwkv_lnx_b, w_br_rwkv, w_out, ln1_g, ln1_b, w_xq, w_xkv, w_xo, ln2_g, ln2_b, w_ff1, w_ff2, ln3_g, ln3_b):
    bsz, seq, d = x.shape
    t = bsz * seq
    n_mem = mem.shape[1]
    depth = w_in.shape[0]

    inv_freq = 1.0 / (ROPE_THETA ** (jnp.arange(0, A_HEAD_DIM, 2, dtype=F32) / A_HEAD_DIM))
    ang = positions.astype(F32).reshape(t, 1) * inv_freq
    cos, sin = jnp.cos(ang), jnp.sin(ang)
    cos_t = jnp.concatenate([cos, cos, cos, cos], axis=-1)
    sin_t = jnp.concatenate([-sin, sin, -sin, sin], axis=-1)

    c_qk = 2 * A_HEADS * 2 * A_HEAD_DIM
    c_v = c_qk + A_HEADS * A_VAL_DIM
    c_pool = c_v + d
    c_rkv = c_pool + 3 * d
    c_lora = c_rkv + DECAY_LORA + AAA_LORA + GATE_LORA

    h32, h16 = _ln_in(x.reshape(t, d), ln_in_g, ln_in_b)
    mem2d = mem.reshape(bsz * n_mem, d)

    for l in range(depth):
        w = w_in[l]
        w_qk = w[:, :c_qk].astype(BF16)
        w_v = w[:, c_qk:c_v].astype(BF16)
        w_pool = w[:, c_v:c_pool].astype(BF16)
        w_rkv = w[:, c_pool:c_rkv].astype(BF16)
        w_lora = _lora_layout(w[:, c_rkv:c_lora]).astype(BF16)
        w_gate = w[:, c_lora:].astype(BF16)

        qk = _proj(h16, w_qk, BF16, rope=(cos_t, sin_t), name="proj_qk").reshape(bsz, seq, c_qk)
        v = _proj(h16, w_v, BF16, name="proj_v").reshape(bsz, seq, d)
        u_pool = _proj(h16, w_pool, BF16, name="proj_pool").reshape(bsz, seq, d)
        u_rkv = _proj(h16, w_rkv, BF16, name="proj_rkv").reshape(bsz, seq, 3 * d)
        u_lora = _proj(h16, w_lora, BF16, name="proj_lora").reshape(bsz, seq, LORA_PAD)
        gates = _proj(h16, w_gate, BF16, gate_bias=b_gate[l].reshape(1, 3 * d), name="proj_gate")

        lam_init = 0.8 - 0.6 * math.exp(-0.3 * l)
        lam_rows = jnp.stack([lam_q1[l], lam_k1[l], lam_q2[l], lam_k2[l]]).astype(F32)
        y_a = _diff_attn(qk, v, lam_rows, attn_subln_g[l], lam_init)

        y_b = _pool(u_pool, pool_w[l].astype(BF16), pool_scale[l])

        mu = rwkv_mu[l]
        mu_rkv = mu[:3 * d].reshape(1, 3 * d)
        mu_lora = _lora_layout(mu[3 * d:]).reshape(1, LORA_PAD)
        vecs = jnp.stack([rwkv_w0[l], rwkv_a0[l], rwkv_k_k[l], rwkv_k_a[l], rwkv_r_k[l].reshape(d),
                          rwkv_lnx_g[l], rwkv_lnx_b[l], jnp.zeros((d,), F32)])
        w2p = _pad_rows(rwkv_w2[l], 128).astype(BF16)
        a2p = _pad_rows(rwkv_a2[l], 128).astype(BF16)
        g2p = _pad_rows(rwkv_g2[l], 256).astype(BF16)
        y_c = _rwkv(u_rkv, u_lora, mu_rkv, mu_lora, vecs, w2p, a2p, g2p)

        h32, h16 = _merge(y_a.reshape(t, d), y_b.reshape(t, d), y_c.reshape(t, d), gates, h32,
                          w_br_attn[l].astype(BF16), w_br_pool[l].astype(BF16), w_br_rwkv[l].astype(BF16),
                          w_out[l].astype(BF16), ln1_g[l], ln1_b[l])

        kv = _proj(mem2d, w_xkv[l].astype(BF16), BF16, name="proj_kv").reshape(bsz, n_mem, 2 * d)
        h32, h16 = _cross(h16.reshape(bsz, seq, d), h32.reshape(bsz, seq, d), kv, w_xq[l].astype(BF16),
                          w_xo[l].astype(BF16), ln2_g[l], ln2_b[l])
        h32, h16 = h32.reshape(t, d), h16.reshape(t, d)

        h32, h16 = _mlp(h16, h32, w_ff1[l].astype(BF16), w_ff2[l].astype(BF16), ln3_g[l], ln3_b[l])

    return h32.reshape(bsz, seq, d)
```

```python
import functools
import math

import jax
import jax.numpy as jnp
from jax import lax
from jax.experimental import pallas as pl
from jax.experimental.pallas import tpu as pltpu

F32 = jnp.float32
BF16 = jnp.bfloat16

D_MODEL = 1024
DEPTH = 2
CHUNK = 64
A_HEADS = 8
A_HEAD_DIM = 64
A_VAL_DIM = 2 * A_HEAD_DIM
ROPE_THETA = 10000.0
POOL_WINDOWS = (2, 4, 8, 16)
POOL_GROUP_DIM = D_MODEL // len(POOL_WINDOWS)
R_HEAD_DIM = 64
DECAY_LORA = 64
AAA_LORA = 64
GATE_LORA = 160
X_HEADS = 4
X_HEAD_DIM = D_MODEL // X_HEADS
D_FF = 4 * D_MODEL
DN_ALPHA = (2 * DEPTH) ** 0.25
LN_EPS = 1e-5
RMS_EPS = 1e-5
RWKV_GN_EPS = 64e-5
NEG_INF = -1e30

V7X_LANES = 128
V7X_SUBLANES = 8
V7X_MXU_DIM = 256
V7X_VMEM_LIMIT_BYTES = 56 * 1024 * 1024

ROW_TILE = 1024
COL_TILE = 1024
FUSED_ROW_TILE = 256
MLP_ROW_TILE = 512
MLP_FF_TILE = 1024
ATTN_Q_TILE = 256
ATTN_K_TILE = 256
ATTN_HEADS_PER_STEP = 2
RW_CHUNK = 64
RW_QUAD = V7X_MXU_DIM
RW_TILE = 128
HALO_ROWS = 16
LORA_PAD = 512
HIGHEST = lax.Precision.HIGHEST
RW_PASSES = 1
RW_SCAN_PASSES = 1


def _tile(n, pref):
    return pref if n % pref == 0 else n


def _params(*sem):
    return pltpu.CompilerParams(dimension_semantics=sem, vmem_limit_bytes=V7X_VMEM_LIMIT_BYTES)


def _ln(z, g, b, eps):
    mu = jnp.mean(z, -1, keepdims=True)
    d = z - mu
    var = jnp.mean(d * d, -1, keepdims=True)
    return d * lax.rsqrt(var + eps) * g + b


def _dot(a, b, precision=None):
    return jnp.dot(a, b, preferred_element_type=F32, precision=precision)


def _dot_nt(a, b, precision=None):
    return lax.dot_general(a, b, (((1,), (1,)), ((), ())), preferred_element_type=F32, precision=precision)


def _dot_tn(a, b, precision=None):
    return lax.dot_general(a, b, (((0,), (0,)), ((), ())), preferred_element_type=F32, precision=precision)


_DIMS = {"nn": (((1,), (0,)), ((), ())), "nt": (((1,), (1,)), ((), ())), "tn": (((0,), (0,)), ((), ()))}


def _split(x):
    hi = x.astype(BF16)
    return hi, (x - hi.astype(F32)).astype(BF16)


def _mm(a, b, kind="nn", passes=None):
    passes = RW_PASSES if passes is None else passes
    dg = functools.partial(lax.dot_general, dimension_numbers=_DIMS[kind], preferred_element_type=F32)
    if passes == 6:
        return dg(a, b, precision=HIGHEST)
    if passes == 1:
        return dg(a.astype(BF16), b.astype(BF16))
    a_hi, a_lo = _split(a)
    b_hi, b_lo = _split(b)
    return dg(a_hi, b_hi) + (dg(a_hi, b_lo) + dg(a_lo, b_hi))


def _ln_in_kernel(x_ref, g_ref, b_ref, h32_ref, h16_ref):
    h = _ln(x_ref[...], g_ref[...], b_ref[...], LN_EPS)
    h32_ref[...] = h
    h16_ref[...] = h.astype(BF16)


def _ln_in(x2d, g, b):
    t, d = x2d.shape
    tm = _tile(t, ROW_TILE)
    row = pl.BlockSpec((tm, d), lambda i: (i, 0))
    vec = pl.BlockSpec((1, d), lambda i: (0, 0))
    return pl.pallas_call(
        _ln_in_kernel,
        grid=(t // tm,),
        in_specs=[row, vec, vec],
        out_specs=[row, row],
        out_shape=[jax.ShapeDtypeStruct((t, d), F32), jax.ShapeDtypeStruct((t, d), BF16)],
        compiler_params=_params("parallel"),
        name="ln_in",
    )(x2d, g.reshape(1, d), b.reshape(1, d))


def _proj_kernel(x_ref, w_ref, o_ref):
    o_ref[...] = _dot(x_ref[...].astype(BF16), w_ref[...]).astype(o_ref.dtype)


def _proj_rope_kernel(x_ref, w_ref, cos_ref, sin_ref, o_ref):
    t = _dot(x_ref[...], w_ref[...])
    cos = cos_ref[...]
    sin = sin_ref[...]
    lane = lax.broadcasted_iota(jnp.int32, cos.shape, 1)
    first_half = (lane & (A_HEAD_DIM - 1)) < A_HEAD_DIM // 2
    for j in range(t.shape[1] // V7X_LANES):
        blk = t[:, j * V7X_LANES:(j + 1) * V7X_LANES]
        partner = jnp.where(first_half,
                            pltpu.roll(blk, V7X_LANES - A_HEAD_DIM // 2, 1),
                            pltpu.roll(blk, A_HEAD_DIM // 2, 1))
        o_ref[:, j * V7X_LANES:(j + 1) * V7X_LANES] = (blk * cos + partner * sin).astype(o_ref.dtype)


def _proj_gate_kernel(x_ref, w_ref, b_ref, o_ref):
    o_ref[...] = jax.nn.sigmoid(_dot(x_ref[...], w_ref[...]) + b_ref[...]).astype(o_ref.dtype)


def _proj(x, w16, out_dtype, rope=None, gate_bias=None, name="proj"):
    m, k = x.shape
    n = w16.shape[1]
    tm = _tile(m, ROW_TILE)
    tn = _tile(n, COL_TILE)
    in_specs = [pl.BlockSpec((tm, k), lambda i, j: (i, 0)), pl.BlockSpec((k, tn), lambda i, j: (0, j))]
    args = [x, w16]
    body = _proj_kernel
    if rope is not None:
        in_specs += [pl.BlockSpec((tm, V7X_LANES), lambda i, j: (i, 0))] * 2
        args += list(rope)
        body = _proj_rope_kernel
    if gate_bias is not None:
        in_specs += [pl.BlockSpec((1, tn), lambda i, j: (0, j))]
        args += [gate_bias]
        body = _proj_gate_kernel
    return pl.pallas_call(
        body,
        grid=(m // tm, n // tn),
        in_specs=in_specs,
        out_specs=pl.BlockSpec((tm, tn), lambda i, j: (i, j)),
        out_shape=jax.ShapeDtypeStruct((m, n), out_dtype),
        compiler_params=_params("parallel", "parallel"),
        name=name,
    )(*args)


def _diff_attn_kernel(lam_ref, g_ref, q_ref, k_ref, v_ref, o_ref, s_buf, p_buf, *, lam_init):
    i = pl.program_id(2)
    tq = q_ref.shape[1]
    tk = s_buf.shape[2]
    n_heads = s_buf.shape[0]
    head_cols = [slice(h * A_VAL_DIM, (h + 1) * A_VAL_DIM) for h in range(n_heads)]

    def stacked_queries(cols):
        q = q_ref[0, :, cols] * jnp.asarray(A_HEAD_DIM ** -0.5, BF16)
        lane = lax.broadcasted_iota(jnp.int32, q.shape, 1)
        zero = jnp.zeros_like(q)
        return jnp.concatenate([jnp.where(lane < A_HEAD_DIM, q, zero), jnp.where(lane >= A_HEAD_DIM, q, zero)], axis=0)

    qs = [stacked_queries(cols) for cols in head_cols]

    def rows(j):
        return pl.ds(pl.multiple_of(j * tk, tk), tk)

    def next_scores(h, j):
        return _dot_nt(k_ref[0, rows(j), head_cols[h]], qs[h])

    def values_product(h, j, p):
        return _dot_tn(v_ref[0, rows(j), head_cols[h]], p)

    def softmax_step(j, carry, s, pv, masked):
        m, l, acc = carry
        if masked:
            kpos = j * tk + lax.broadcasted_iota(jnp.int32, (tk, 2 * tq), 0)
            qpos = i * tq + (lax.broadcasted_iota(jnp.int32, (tk, 2 * tq), 1) & (tq - 1))
            chunk_shift = CHUNK.bit_length() - 1
            s = jnp.where((kpos >> chunk_shift) <= (qpos >> chunk_shift), s, NEG_INF)
        m_new = jnp.maximum(m, jnp.max(s, axis=0, keepdims=True))
        alpha = jnp.exp(m - m_new)
        p = jnp.exp(s - m_new)
        l = alpha * l + jnp.sum(p.reshape(tk // V7X_SUBLANES, V7X_SUBLANES, 2 * tq), axis=0)
        return (m_new, l, alpha * (acc + pv)), p.astype(BF16)

    def body(j, carries):
        slot = j & 1
        loaded = [(s_buf[h, slot], p_buf[h, 1 - slot]) for h in range(n_heads)]
        pvs = [values_product(h, jnp.maximum(j - 1, 0), loaded[h][1]) for h in range(n_heads)]
        s_next = [next_scores(h, j + 1) for h in range(n_heads)]
        stepped = [softmax_step(j, carries[h], loaded[h][0], pvs[h], False) for h in range(n_heads)]
        for h in range(n_heads):
            s_buf[h, 1 - slot] = s_next[h]
            p_buf[h, slot] = stepped[h][1]
        return tuple(c for c, _ in stepped)

    for h in range(n_heads):
        s_buf[h, 0] = next_scores(h, 0)
        p_buf[h, 1] = jnp.zeros(p_buf.shape[2:], BF16)
    init = (jnp.full((1, 2 * tq), NEG_INF, F32), jnp.zeros((V7X_SUBLANES, 2 * tq), F32),
            jnp.zeros((A_VAL_DIM, 2 * tq), F32))
    n_full = (i * tq) // tk
    carries = lax.fori_loop(0, n_full, body, (init,) * n_heads)

    lam_rows = lam_ref[...]
    lam = (jnp.exp(jnp.sum(lam_rows[0:1] * lam_rows[1:2], -1, keepdims=True))
           - jnp.exp(jnp.sum(lam_rows[2:3] * lam_rows[3:4], -1, keepdims=True)) + lam_init)
    last = n_full & 1
    for h in range(n_heads):
        pv = values_product(h, jnp.maximum(n_full - 1, 0), p_buf[h, 1 - last])
        (_, l, acc), p = softmax_step(n_full, carries[h], s_buf[h, last], pv, True)
        acc = acc + values_product(h, n_full, p)
        out = acc / jnp.sum(l, axis=0, keepdims=True)
        o = out[:, :tq] - lam * out[:, tq:]
        o = o * lax.rsqrt(jnp.mean(o * o, axis=0, keepdims=True) + RMS_EPS) * g_ref[...]
        o_ref[0, :, head_cols[h]] = (o * (1.0 - lam_init)).T.astype(o_ref.dtype)


def _diff_attn(qk, v, lam_rows, subln_g, lam_init):
    b, s, _ = v.shape
    tq = _tile(s, ATTN_Q_TILE)
    tk = min(ATTN_K_TILE, s)
    nh = ATTN_HEADS_PER_STEP
    width = nh * A_VAL_DIM
    k_blk0 = A_HEADS // nh
    return pl.pallas_call(
        functools.partial(_diff_attn_kernel, lam_init=lam_init),
        grid=(b, A_HEADS // nh, s // tq),
        in_specs=[
            pl.BlockSpec((4, A_HEAD_DIM), lambda bi, h, i: (0, 0)),
            pl.BlockSpec((A_VAL_DIM, 1), lambda bi, h, i: (0, 0)),
            pl.BlockSpec((1, tq, width), lambda bi, h, i: (bi, i, h)),
            pl.BlockSpec((1, s, width), lambda bi, h, i: (bi, 0, k_blk0 + h)),
            pl.BlockSpec((1, s, width), lambda bi, h, i: (bi, 0, h)),
        ],
        out_specs=pl.BlockSpec((1, tq, width), lambda bi, h, i: (bi, i, h)),
        out_shape=jax.ShapeDtypeStruct(v.shape, BF16),
        scratch_shapes=[pltpu.VMEM((nh, 2, tk, 2 * tq), F32), pltpu.VMEM((nh, 2, tk, 2 * tq), BF16)],
        compiler_params=_params("parallel", "parallel", "arbitrary"),
        name="diff_attn",
    )(lam_rows, subln_g.reshape(A_VAL_DIM, 1), qk, qk, v)


def _pool_kernel(p_ref, w_ref, scale_ref, o_ref):
    g = pl.program_id(1)
    x = p_ref[0].astype(F32)
    row = lax.broadcasted_iota(jnp.int32, x.shape, 0)
    for gi, window in enumerate(POOL_WINDOWS):
        @pl.when(g == gi)
        def _(window=window):
            ws = x
            shift = 1
            while shift < window:
                ws = ws + jnp.where(row >= shift, pltpu.roll(ws, shift, 0), 0.0)
                shift *= 2
            count = jnp.minimum(row + 1, window).astype(F32)
            d = ws / count - x
            y = _dot(d.astype(BF16), w_ref[0]) * scale_ref[...]
            o_ref[0] = y.astype(o_ref.dtype)


def _pool(u_pool, pool_w16, pool_scale):
    b, s, d = u_pool.shape
    cg = POOL_GROUP_DIM
    return pl.pallas_call(
        _pool_kernel,
        grid=(b, len(POOL_WINDOWS)),
        in_specs=[
            pl.BlockSpec((1, s, cg), lambda bi, g: (bi, 0, g)),
            pl.BlockSpec((1, cg, cg), lambda bi, g: (g, 0, 0)),
            pl.BlockSpec((1, cg), lambda bi, g: (0, g)),
        ],
        out_specs=pl.BlockSpec((1, s, cg), lambda bi, g: (bi, 0, g)),
        out_shape=jax.ShapeDtypeStruct((b, s, d), BF16),
        compiler_params=_params("parallel", "parallel"),
        name="pool_mixer",
    )(u_pool, pool_w16, pool_scale.reshape(1, d))


def _block_mask():
    r = lax.broadcasted_iota(jnp.int32, (RW_QUAD, RW_QUAD), 0)
    c = lax.broadcasted_iota(jnp.int32, (RW_QUAD, RW_QUAD), 1)
    return (r // R_HEAD_DIM) == (c // R_HEAD_DIM)


def _bd(x, mask):
    tiled = jnp.concatenate([x] * (RW_QUAD // R_HEAD_DIM), axis=0)
    return jnp.where(mask, tiled, jnp.zeros_like(tiled))


def _seg_quad(x, ones_bd):
    hi, lo = _split(x)
    return _dot(hi, ones_bd) + _dot(lo, ones_bd)


def _seg_sum(x, ones_bd):
    parts = [_seg_quad(x[:, q * RW_QUAD:(q + 1) * RW_QUAD], ones_bd) for q in range(x.shape[1] // RW_QUAD)]
    return jnp.concatenate(parts, axis=1)


def _chunk_summaries(units, mask):
    c = RW_CHUNK
    ii = lax.broadcasted_iota(jnp.int32, (c, RW_QUAD), 0)
    jj = lax.broadcasted_iota(jnp.int32, (c, RW_QUAD), 1) % c
    strict = ii > jj
    incl = ii >= jj
    eye = (ii == jj).astype(F32)
    bd = lambda x: _bd(x, mask)

    scaled = []
    for r, k, v, a, b, lw, cum in units:
        cum_end = cum[c - 1:c, :]
        at = a * jnp.exp(cum - lw)
        rt = r * jnp.exp(cum)
        e_neg = jnp.exp(-cum)
        e_end = jnp.exp(cum_end - cum)
        scaled.append((at, rt, b * e_neg, k * e_neg, b * e_end, k * e_end, v, jnp.exp(cum_end)))

    scores = [_mm(jnp.concatenate([at, rt], axis=0), jnp.concatenate([bd(bt), bd(kt)], axis=0), "nt")
              for at, rt, bt, kt, _, _, _, _ in scaled]
    a_ab = [jnp.where(strict, sc[:c, :RW_QUAD], 0.0) for sc in scores]
    a_rb = [jnp.where(incl, sc[c:, :RW_QUAD], 0.0) for sc in scores]
    av = [_mm(jnp.concatenate([jnp.where(strict, sc[:c, RW_QUAD:], 0.0), jnp.where(incl, sc[c:, RW_QUAD:], 0.0)],
                              axis=0), bd(u[6])) for sc, u in zip(scores, scaled)]

    tinv = [a + eye for a in a_ab]
    pk = a_ab
    for _ in range(int(math.log2(c)) - 1):
        pk = [_mm(p, bd(p)) for p in pk]
        tinv = [t + _mm(p, bd(t)) for t, p in zip(tinv, pk)]

    out = []
    for t, arb, avu, (at, rt, _, _, bh, kh, v, w_end) in zip(tinv, a_rb, av, scaled):
        xa = _mm(t, bd(at))
        xu = _mm(t, bd(avu[:c]))
        rp = rt + _mm(arb, bd(xa))
        y0 = avu[c:] + _mm(arb, bd(xu))
        z1 = jnp.where(mask, _mm(bh, xa, "tn"), 0.0)
        z2 = jnp.where(mask, _mm(bh, xu, "tn") + _mm(kh, v, "tn"), 0.0)
        fold = lambda z: functools.reduce(jnp.add, [z[h * c:(h + 1) * c] for h in range(RW_QUAD // R_HEAD_DIM)])
        out.append((rp, y0, fold(z1) + eye * w_end, fold(z2)))
    return out


def _rwkv_kernel(rkv_ref, rkv_halo_ref, lora_ref, lora_halo_ref, mu_rkv_ref, mu_lora_ref, vec_ref,
                 w2_ref, a2_ref, g2_ref, o_ref, state_ref):
    t = pl.program_id(1)
    ts = rkv_ref.shape[1]
    d = D_MODEL

    @pl.when(t == 0)
    def _():
        state_ref[...] = jnp.zeros_like(state_ref)

    def shifted(x_ref, halo_ref, mu):
        x = x_ref[0].astype(F32)
        halo = halo_ref[0].astype(F32)
        row = lax.broadcasted_iota(jnp.int32, x.shape, 0)
        before = jnp.where(t > 0, halo[HALO_ROWS - 1:HALO_ROWS, :], 0.0)
        prev = jnp.where(row == 0, before, pltpu.roll(x, 1, 0))
        return x + (prev - x) * mu

    xs = shifted(rkv_ref, rkv_halo_ref, mu_rkv_ref[...])
    ls = shifted(lora_ref, lora_halo_ref, mu_lora_ref[...])
    r = xs[:, :d]
    k = xs[:, d:2 * d]
    v = xs[:, 2 * d:]
    vec = vec_ref[...]
    w0, a0, k_k, k_a, r_k = vec[0:1], vec[1:2], vec[2:3], vec[3:4], vec[4:5]

    z = w0 + _dot(jnp.tanh(ls[:, :128]).astype(BF16), w2_ref[...])
    lw = -math.exp(-0.5) * jax.nn.sigmoid(z)
    a_sig = jax.nn.sigmoid(a0 + _dot(ls[:, 128:256].astype(BF16), a2_ref[...]))
    gate = _dot(jax.nn.sigmoid(ls[:, 256:]).astype(BF16), g2_ref[...])

    mask = _block_mask()
    ones_bd = mask.astype(BF16)
    kk = k * k_k
    kk = kk / jnp.maximum(jnp.sqrt(_seg_sum(kk * kk, ones_bd)), 1e-12)
    k2 = k * (1.0 + (a_sig - 1.0) * k_a)
    a_vec = -kk
    b_vec = kk * a_sig
    bonus = _seg_sum(r * k2 * r_k, ones_bd) * v

    row_in_chunk = lax.broadcasted_iota(jnp.int32, lw.shape, 0) % RW_CHUNK
    cum = lw
    shift = 1
    while shift < RW_CHUNK:
        cum = cum + jnp.where(row_in_chunk >= shift, pltpu.roll(cum, shift, 0), 0.0)
        shift *= 2

    windows = [(slice(c * RW_CHUNK, (c + 1) * RW_CHUNK), slice(q * RW_QUAD, (q + 1) * RW_QUAD))
               for c in range(ts // RW_CHUNK) for q in range(d // RW_QUAD)]
    units = [tuple(x[rows, cols] for x in (r, k2, v, a_vec, b_vec, lw, cum)) for rows, cols in windows]
    summaries = _chunk_summaries(units, mask)

    inv_n = 1.0 / R_HEAD_DIM
    lnx_g, lnx_b = vec[5:6], vec[6:7]
    n_quads = d // RW_QUAD
    states = [state_ref[:, q * RW_QUAD:(q + 1) * RW_QUAD] for q in range(n_quads)]
    for (rows, cols), (rp, y0, m, n) in zip(windows, summaries):
        q = cols.start // RW_QUAD
        out = _mm(jnp.concatenate([rp, m], axis=0), _bd(states[q], mask), passes=RW_SCAN_PASSES)
        y = out[:RW_CHUNK] + y0
        states[q] = out[RW_CHUNK:] + n
        mu = _seg_quad(y, ones_bd) * inv_n
        dy = y - mu
        var = _seg_quad(dy * dy, ones_bd) * inv_n
        yn = dy * lax.rsqrt(var + RWKV_GN_EPS) * lnx_g[:, cols] + lnx_b[:, cols]
        o_ref[0, rows, cols] = ((yn + bonus[rows, cols]) * gate[rows, cols]).astype(o_ref.dtype)
    for q in range(n_quads):
        state_ref[:, q * RW_QUAD:(q + 1) * RW_QUAD] = states[q]


def _rwkv(u_rkv, u_lora, mu_rkv, mu_lora, vecs, w2p, a2p, g2p):
    b, s, w3 = u_rkv.shape
    d = D_MODEL
    ts = _tile(s, RW_TILE)
    halo_blocks = ts // HALO_ROWS

    def tile_spec(width):
        return pl.BlockSpec((1, ts, width), lambda bi, t: (bi, t, 0))

    def halo_spec(width):
        return pl.BlockSpec((1, HALO_ROWS, width), lambda bi, t: (bi, jnp.maximum(t * halo_blocks - 1, 0), 0))

    def const_spec(shape):
        return pl.BlockSpec(shape, lambda bi, t: (0,) * len(shape))

    return pl.pallas_call(
        _rwkv_kernel,
        grid=(b, s // ts),
        in_specs=[tile_spec(w3), halo_spec(w3), tile_spec(LORA_PAD), halo_spec(LORA_PAD),
                  const_spec((1, w3)), const_spec((1, LORA_PAD)), const_spec(vecs.shape),
                  const_spec(w2p.shape), const_spec(a2p.shape), const_spec(g2p.shape)],
        out_specs=tile_spec(d),
        out_shape=jax.ShapeDtypeStruct((b, s, d), BF16),
        scratch_shapes=[pltpu.VMEM((R_HEAD_DIM, d), F32)],
        compiler_params=_params("parallel", "arbitrary"),
        name="rwkv7",
    )(u_rkv, u_rkv, u_lora, u_lora, mu_rkv, mu_lora, vecs, w2p, a2p, g2p)


def _merge_kernel(ya_ref, yb_ref, yc_ref, gates_ref, h_ref, wa_ref, wb_ref, wc_ref, wo_ref,
                  g_ref, b_ref, h32_ref, h16_ref):
    d = D_MODEL
    merged = jnp.zeros((ya_ref.shape[0], d), F32)
    for idx, (y_ref, w_ref) in enumerate(((ya_ref, wa_ref), (yb_ref, wb_ref), (yc_ref, wc_ref))):
        gate = gates_ref[:, idx * d:(idx + 1) * d].astype(F32)
        merged = merged + gate * _dot(y_ref[...], w_ref[...])
    z = DN_ALPHA * h_ref[...] + _dot(merged.astype(BF16), wo_ref[...])
    h = _ln(z, g_ref[...], b_ref[...], LN_EPS)
    h32_ref[...] = h
    h16_ref[...] = h.astype(BF16)


def _merge(ya, yb, yc, gates, h32, wa, wb, wc, wo, ln_g, ln_b):
    t, d = h32.shape
    tm = _tile(t, FUSED_ROW_TILE)
    row = pl.BlockSpec((tm, d), lambda i: (i, 0))
    wspec = pl.BlockSpec((d, d), lambda i: (0, 0))
    vec = pl.BlockSpec((1, d), lambda i: (0, 0))
    return pl.pallas_call(
        _merge_kernel,
        grid=(t // tm,),
        in_specs=[row, row, row, pl.BlockSpec((tm, 3 * d), lambda i: (i, 0)),
                  row, wspec, wspec, wspec, wspec, vec, vec],
        out_specs=[row, row],
        out_shape=[jax.ShapeDtypeStruct((t, d), F32), jax.ShapeDtypeStruct((t, d), BF16)],
        compiler_params=_params("parallel"),
        name="merge_out",
    )(ya, yb, yc, gates, h32, wa, wb, wc, wo, ln_g.reshape(1, d), ln_b.reshape(1, d))


def _cross_kernel(h16_ref, h32_ref, kv_ref, wq_ref, wo_ref, g_ref, b_ref, o32_ref, o16_ref):
    d = D_MODEL
    q = _dot(h16_ref[0], wq_ref[...]).astype(BF16)
    kv = kv_ref[0]
    heads = []
    for hd in range(X_HEADS):
        cols = slice(hd * X_HEAD_DIM, (hd + 1) * X_HEAD_DIM)
        s = _dot_nt(q[:, cols], kv[:, cols]) * (X_HEAD_DIM ** -0.5)
        p = jnp.exp(s - jnp.max(s, -1, keepdims=True))
        o = _dot(p.astype(BF16), kv[:, d + hd * X_HEAD_DIM:d + (hd + 1) * X_HEAD_DIM])
        heads.append((o / jnp.sum(p, -1, keepdims=True)).astype(BF16))
    attn = jnp.concatenate(heads, axis=1)
    z = DN_ALPHA * h32_ref[0] + _dot(attn, wo_ref[...])
    h = _ln(z, g_ref[...], b_ref[...], LN_EPS)
    o32_ref[0] = h
    o16_ref[0] = h.astype(BF16)


def _cross(h16, h32, kv, wq, wo, ln_g, ln_b):
    b, s, d = h32.shape
    tm = _tile(s, FUSED_ROW_TILE)
    row = pl.BlockSpec((1, tm, d), lambda bi, i: (bi, i, 0))
    wspec = pl.BlockSpec((d, d), lambda bi, i: (0, 0))
    vec = pl.BlockSpec((1, d), lambda bi, i: (0, 0))
    n_mem = kv.shape[1]
    return pl.pallas_call(
        _cross_kernel,
        grid=(b, s // tm),
        in_specs=[row, row, pl.BlockSpec((1, n_mem, 2 * d), lambda bi, i: (bi, 0, 0)), wspec, wspec, vec, vec],
        out_specs=[row, row],
        out_shape=[jax.ShapeDtypeStruct((b, s, d), F32), jax.ShapeDtypeStruct((b, s, d), BF16)],
        compiler_params=_params("parallel", "parallel"),
        name="cross_attn",
    )(h16, h32, kv, wq, wo, ln_g.reshape(1, d), ln_b.reshape(1, d))


def _mlp_kernel(h16_ref, h32_ref, w1_ref, w2_ref, g_ref, b_ref, o32_ref, o16_ref, acc_ref):
    f = pl.program_id(1)

    @pl.when(f == 0)
    def _():
        acc_ref[...] = jnp.zeros_like(acc_ref)

    a = jnp.maximum(_dot(h16_ref[...], w1_ref[...]), 0.0)
    acc_ref[...] += _dot((a * a).astype(BF16), w2_ref[...])

    @pl.when(f == pl.num_programs(1) - 1)
    def _():
        h = _ln(DN_ALPHA * h32_ref[...] + acc_ref[...], g_ref[...], b_ref[...], LN_EPS)
        o32_ref[...] = h
        o16_ref[...] = h.astype(BF16)


def _mlp(h16, h32, w1, w2, ln_g, ln_b):
    t, d = h32.shape
    dff = w1.shape[1]
    tm = _tile(t, MLP_ROW_TILE)
    tf = _tile(dff, MLP_FF_TILE)
    row = pl.BlockSpec((tm, d), lambda i, f: (i, 0))
    vec = pl.BlockSpec((1, d), lambda i, f: (0, 0))
    return pl.pallas_call(
        _mlp_kernel,
        grid=(t // tm, dff // tf),
        in_specs=[row, row, pl.BlockSpec((d, tf), lambda i, f: (0, f)), pl.BlockSpec((tf, d), lambda i, f: (f, 0)),
                  vec, vec],
        out_specs=[row, row],
        out_shape=[jax.ShapeDtypeStruct((t, d), F32), jax.ShapeDtypeStruct((t, d), BF16)],
        scratch_shapes=[pltpu.VMEM((tm, d), F32)],
        compiler_params=_params("parallel", "arbitrary"),
        name="mlp",
    )(h16, h32, w1, w2, ln_g.reshape(1, d), ln_b.reshape(1, d))


def _pad_rows(w, rows):
    return jnp.pad(w, ((0, rows - w.shape[0]), (0, 0)))


def _lora_layout(w):
    pad = lambda t, n: jnp.pad(t, [(0, 0)] * (t.ndim - 1) + [(0, n - t.shape[-1])])
    xw = w[..., :DECAY_LORA]
    xa = w[..., DECAY_LORA:DECAY_LORA + AAA_LORA]
    xg = w[..., DECAY_LORA + AAA_LORA:]
    return jnp.concatenate([pad(xw, 128), pad(xa, 128), pad(xg, 256)], axis=-1)


def kernel(x, mem, positions, ln_in_g, ln_in_b, w_in, b_gate, lam_q1, lam_k1, lam_q2, lam_k2, attn_subln_g, w_br_attn, pool_w, pool_scale, w_br_pool, rwkv_mu, rwkv_w0, rwkv_w2, rwkv_a0, rwkv_a2, rwkv_g2, rwkv_k_k, rwkv_k_a, rwkv_r_k, rwkv_lnx_g, rwkv_lnx_b, w_br_rwkv, w_out, ln1_g, ln1_b, w_xq, w_xkv, w_xo, ln2_g, ln2_b, w_ff1, w_ff2, ln3_g, ln3_b):
    bsz, seq, d = x.shape
    t = bsz * seq
    n_mem = mem.shape[1]
    depth = w_in.shape[0]

    inv_freq = 1.0 / (ROPE_THETA ** (jnp.arange(0, A_HEAD_DIM, 2, dtype=F32) / A_HEAD_DIM))
    ang = positions.astype(F32).reshape(t, 1) * inv_freq
    cos, sin = jnp.cos(ang), jnp.sin(ang)
    cos_t = jnp.concatenate([cos, cos, cos, cos], axis=-1)
    sin_t = jnp.concatenate([-sin, sin, -sin, sin], axis=-1)

    c_qk = 2 * A_HEADS * 2 * A_HEAD_DIM
    c_v = c_qk + A_HEADS * A_VAL_DIM
    c_pool = c_v + d
    c_rkv = c_pool + 3 * d
    c_lora = c_rkv + DECAY_LORA + AAA_LORA + GATE_LORA

    h32, h16 = _ln_in(x.reshape(t, d), ln_in_g, ln_in_b)
    mem2d = mem.reshape(bsz * n_mem, d)

    for l in range(depth):
        w = w_in[l]
        w_qk = w[:, :c_qk].astype(BF16)
        w_v = w[:, c_qk:c_v].astype(BF16)
        w_pool = w[:, c_v:c_pool].astype(BF16)
        w_rkv = w[:, c_pool:c_rkv].astype(BF16)
        w_lora = _lora_layout(w[:, c_rkv:c_lora]).astype(BF16)
        w_gate = w[:, c_lora:].astype(BF16)

        qk = _proj(h16, w_qk, BF16, rope=(cos_t, sin_t), name="proj_qk").reshape(bsz, seq, c_qk)
        v = _proj(h16, w_v, BF16, name="proj_v").reshape(bsz, seq, d)
        u_pool = _proj(h16, w_pool, BF16, name="proj_pool").reshape(bsz, seq, d)
        u_rkv = _proj(h16, w_rkv, BF16, name="proj_rkv").reshape(bsz, seq, 3 * d)
        u_lora = _proj(h16, w_lora, BF16, name="proj_lora").reshape(bsz, seq, LORA_PAD)
        gates = _proj(h16, w_gate, BF16, gate_bias=b_gate[l].reshape(1, 3 * d), name="proj_gate")

        lam_init = 0.8 - 0.6 * math.exp(-0.3 * l)
        lam_rows = jnp.stack([lam_q1[l], lam_k1[l], lam_q2[l], lam_k2[l]]).astype(F32)
        y_a = _diff_attn(qk, v, lam_rows, attn_subln_g[l], lam_init)

        y_b = _pool(u_pool, pool_w[l].astype(BF16), pool_scale[l])

        mu = rwkv_mu[l]
        mu_rkv = mu[:3 * d].reshape(1, 3 * d)
        mu_lora = _lora_layout(mu[3 * d:]).reshape(1, LORA_PAD)
        vecs = jnp.stack([rwkv_w0[l], rwkv_a0[l], rwkv_k_k[l], rwkv_k_a[l], rwkv_r_k[l].reshape(d),
                          rwkv_lnx_g[l], rwkv_lnx_b[l], jnp.zeros((d,), F32)])
        w2p = _pad_rows(rwkv_w2[l], 128).astype(BF16)
        a2p = _pad_rows(rwkv_a2[l], 128).astype(BF16)
        g2p = _pad_rows(rwkv_g2[l], 256).astype(BF16)
        y_c = _rwkv(u_rkv, u_lora, mu_rkv, mu_lora, vecs, w2p, a2p, g2p)

        h32, h16 = _merge(y_a.reshape(t, d), y_b.reshape(t, d), y_c.reshape(t, d), gates, h32,
                          w_br_attn[l].astype(BF16), w_br_pool[l].astype(BF16), w_br_rwkv[l].astype(BF16),
                          w_out[l].astype(BF16), ln1_g[l], ln1_b[l])

        kv = _proj(mem2d, w_xkv[l].astype(BF16), BF16, name="proj_kv").reshape(bsz, n_mem, 2 * d)
        h32, h16 = _cross(h16.reshape(bsz, seq, d), h32.reshape(bsz, seq, d), kv, w_xq[l].astype(BF16),
                          w_xo[l].astype(BF16), ln2_g[l], ln2_b[l])
        h32, h16 = h32.reshape(t, d), h16.reshape(t, d)

        h32, h16 = _mlp(h16, h32, w_ff1[l].astype(BF16), w_ff2[l].astype(BF16), ln3_g[l], ln3_b[l])

    return h32.reshape(bsz, seq, d)
```

```python
import functools
import math

import jax
import jax.numpy as jnp
from jax import lax
from jax.experimental import pallas as pl
from jax.experimental.pallas import tpu as pltpu

F32 = jnp.float32
BF16 = jnp.bfloat16

D_MODEL = 1024
DEPTH = 2
CHUNK = 64
A_HEADS = 8
A_HEAD_DIM = 64
A_VAL_DIM = 2 * A_HEAD_DIM
ROPE_THETA = 10000.0
POOL_WINDOWS = (2, 4, 8, 16)
POOL_GROUP_DIM = D_MODEL // len(POOL_WINDOWS)
R_HEAD_DIM = 64
DECAY_LORA = 64
AAA_LORA = 64
GATE_LORA = 160
X_HEADS = 4
X_HEAD_DIM = D_MODEL // X_HEADS
D_FF = 4 * D_MODEL
DN_ALPHA = (2 * DEPTH) ** 0.25
LN_EPS = 1e-5
RMS_EPS = 1e-5
RWKV_GN_EPS = 64e-5
KK_NORM_EPS = 1e-12
NEG_INF = -1e30
LOG2_E = math.log2(math.e)
ONES_ROWS = 16

V7X_LANES = 128
V7X_SUBLANES = 8
V7X_MXU_DIM = 256
V7X_VMEM_LIMIT_BYTES = 56 * 1024 * 1024

ROW_TILE = 1024
COL_TILE = 1024
IN_PROJ_ROW_TILE = 512
IN_PROJ_CHUNK = 1024
FUSED_ROW_TILE = 512
MLP_ROW_TILE = 512
MLP_FF_TILE = 1024
ATTN_Q_TILE = 256
ATTN_K_TILE = 256
ATTN_HEADS_PER_STEP = 2
RW_CHUNK = 64
RW_QUAD = V7X_MXU_DIM
RW_TILE = 256
HALO_ROWS = 16
LORA_PAD = 512


def _tile(n, pref):
    return pref if n % pref == 0 else n


def _params(*sem):
    return pltpu.CompilerParams(dimension_semantics=sem, vmem_limit_bytes=V7X_VMEM_LIMIT_BYTES)


def _ln(z, g, b, eps):
    mu = jnp.mean(z, -1, keepdims=True)
    d = z - mu
    var = jnp.mean(d * d, -1, keepdims=True)
    return d * lax.rsqrt(var + eps) * g + b


_DIMS = {"nn": (((1,), (0,)), ((), ())), "nt": (((1,), (1,)), ((), ())), "tn": (((0,), (0,)), ((), ()))}


def _dot(a, b):
    return jnp.dot(a, b, preferred_element_type=F32)


def _dot_nt(a, b):
    return lax.dot_general(a, b, _DIMS["nt"], preferred_element_type=F32)


def _split(x):
    hi = x.astype(BF16)
    return hi, (x - hi.astype(F32)).astype(BF16)


def _mm(a, b, kind="nn"):
    return lax.dot_general(a.astype(BF16), b.astype(BF16), _DIMS[kind], preferred_element_type=F32)


def _ln_in_kernel(x_ref, g_ref, b_ref, h32_ref, h16_ref):
    h = _ln(x_ref[...], g_ref[...], b_ref[...], LN_EPS)
    h32_ref[...] = h
    h16_ref[...] = h.astype(BF16)


def _ln_in(x2d, g, b):
    t, d = x2d.shape
    tm = _tile(t, ROW_TILE)
    row = pl.BlockSpec((tm, d), lambda i: (i, 0))
    vec = pl.BlockSpec((1, d), lambda i: (0, 0))
    return pl.pallas_call(
        _ln_in_kernel,
        grid=(t // tm,),
        in_specs=[row, vec, vec],
        out_specs=[row, row],
        out_shape=[jax.ShapeDtypeStruct((t, d), F32), jax.ShapeDtypeStruct((t, d), BF16)],
        compiler_params=_params("parallel"),
        name="ln_in",
    )(x2d, g.reshape(1, d), b.reshape(1, d))


def _proj_kernel(x_ref, w_ref, o_ref):
    o_ref[...] = _dot(x_ref[...].astype(BF16), w_ref[...]).astype(o_ref.dtype)


def _proj(x, w16, out_dtype, name):
    m, k = x.shape
    n = w16.shape[1]
    tm = _tile(m, ROW_TILE)
    tn = _tile(n, COL_TILE)
    return pl.pallas_call(
        _proj_kernel,
        grid=(m // tm, n // tn),
        in_specs=[pl.BlockSpec((tm, k), lambda i, j: (i, 0)), pl.BlockSpec((k, tn), lambda i, j: (0, j))],
        out_specs=pl.BlockSpec((tm, tn), lambda i, j: (i, j)),
        out_shape=jax.ShapeDtypeStruct((m, n), out_dtype),
        compiler_params=_params("parallel", "parallel"),
        name=name,
    )(x, w16)


def _rope(t, cos, sin, first_half):
    partner = jnp.where(first_half, pltpu.roll(t, V7X_LANES - A_HEAD_DIM // 2, 1), pltpu.roll(t, A_HEAD_DIM // 2, 1))
    return t * cos + partner * sin


def _in_proj_kernel(x_ref, halo_ref, w_main_ref, w_lora_ref, w_gate_ref, cos_ref, sin_ref, bias_ref, mu_ref,
                    qk_ref, v_ref, pool_ref, rkv_ref, lora_ref, gate_ref, *, tiles_per_seq):
    x = x_ref[...]
    at_start = pl.program_id(0) % tiles_per_seq == 0
    halo = halo_ref[...]
    x_ext = jnp.concatenate([jnp.where(at_start, jnp.zeros_like(halo), halo), x], axis=0)
    cos = cos_ref[...]
    sin = sin_ref[...]
    lane = lax.broadcasted_iota(jnp.int32, cos.shape, 1)
    first_half = (lane & (A_HEAD_DIM - 1)) < A_HEAD_DIM // 2
    main_col = 0
    mu_col = 0
    for o_ref in (qk_ref, v_ref, pool_ref, rkv_ref, lora_ref, gate_ref):
        width = A_HEADS * A_VAL_DIM if o_ref is v_ref else o_ref.shape[1]
        if o_ref is lora_ref:
            w_ref, col = w_lora_ref, 0
        elif o_ref is gate_ref:
            w_ref, col = w_gate_ref, 0
        else:
            w_ref, col = w_main_ref, main_col
        for c0 in range(0, width, IN_PROJ_CHUNK):
            c1 = min(c0 + IN_PROJ_CHUNK, width)
            if o_ref is rkv_ref or o_ref is lora_ref:
                t_ext = _dot(x_ext, w_ref[:, col + c0:col + c1])
                t = t_ext[HALO_ROWS:]
                prev = pltpu.roll(t_ext, 1, 0)[HALO_ROWS:]
                mu = mu_ref[:, mu_col + c0:mu_col + c1]
                o_ref[:, c0:c1] = (t + (prev - t) * mu).astype(o_ref.dtype)
                continue
            t = _dot(x, w_ref[:, col + c0:col + c1])
            if o_ref is qk_ref:
                for j in range(c0, c1, V7X_LANES):
                    blk = t[:, j - c0:j - c0 + V7X_LANES]
                    o_ref[:, j:j + V7X_LANES] = _rope(blk, cos, sin, first_half).astype(o_ref.dtype)
            elif o_ref is gate_ref:
                o_ref[:, c0:c1] = jax.nn.sigmoid(t + bias_ref[:, c0:c1]).astype(o_ref.dtype)
            elif o_ref is v_ref:
                tk = v_ref.shape[4]
                for h in range((c1 - c0) // A_VAL_DIM):
                    for kt in range(v_ref.shape[2]):
                        blk = t[kt * tk:(kt + 1) * tk, h * A_VAL_DIM:(h + 1) * A_VAL_DIM]
                        head = c0 // A_VAL_DIM + h
                        v_ref[0, head, kt, :A_VAL_DIM, :] = blk.T.astype(v_ref.dtype)
                        v_ref[0, head, kt, A_VAL_DIM:, :] = jnp.ones((ONES_ROWS, tk), v_ref.dtype)
            else:
                o_ref[:, c0:c1] = t.astype(o_ref.dtype)
        if w_ref is w_main_ref:
            main_col += width
        if o_ref is rkv_ref or o_ref is lora_ref:
            mu_col += width


def _in_proj(x16, weights, cos_t, sin_t, gate_bias, shift_mu, widths, seq, tk):
    m, k = x16.shape
    tm = _tile(seq, IN_PROJ_ROW_TILE)
    tiles_per_seq = seq // tm
    row = lambda n: pl.BlockSpec((tm, n), lambda i: (i, 0))
    vt_rows = A_VAL_DIM + ONES_ROWS
    vt_spec = pl.BlockSpec((1, A_HEADS, tm // tk, vt_rows, tk),
                           lambda i: (i // tiles_per_seq, 0, i % tiles_per_seq, 0, 0))
    out_specs = [row(n) for n in widths]
    out_shape = [jax.ShapeDtypeStruct((m, n), BF16) for n in widths]
    out_specs[1] = vt_spec
    out_shape[1] = jax.ShapeDtypeStruct((m // seq, A_HEADS, seq // tk, vt_rows, tk), BF16)
    halo_blocks = tm // HALO_ROWS
    halo = pl.BlockSpec((HALO_ROWS, k), lambda i: (jnp.maximum(i * halo_blocks - 1, 0), 0))
    const = lambda a: pl.BlockSpec(a.shape, lambda i: (0, 0))
    return pl.pallas_call(
        functools.partial(_in_proj_kernel, tiles_per_seq=tiles_per_seq),
        grid=(m // tm,),
        in_specs=[row(k), halo] + [pl.BlockSpec(w.shape, lambda i: (0, 0), pipeline_mode=pl.Buffered(1)) for w in weights]
        + [row(V7X_LANES), row(V7X_LANES), const(gate_bias), const(shift_mu)],
        out_specs=out_specs,
        out_shape=out_shape,
        compiler_params=_params("parallel"),
        name="in_proj",
    )(x16, x16, *weights, cos_t, sin_t, gate_bias, shift_mu)


def _diff_attn_kernel(lam_ref, g_ref, q_ref, k_ref, v_ref, o_ref, s_buf, p_buf, *, lam_init):
    i = pl.program_id(2)
    tq = q_ref.shape[1]
    tk = s_buf.shape[2]
    n_heads = s_buf.shape[0]
    head_cols = [slice(h * A_VAL_DIM, (h + 1) * A_VAL_DIM) for h in range(n_heads)]

    def stacked_queries(cols):
        q = (q_ref[0, :, cols].astype(F32) * (A_HEAD_DIM ** -0.5 * LOG2_E)).astype(BF16)
        lane = lax.broadcasted_iota(jnp.int32, q.shape, 1)
        zero = jnp.zeros_like(q)
        return jnp.concatenate([jnp.where(lane < A_HEAD_DIM, q, zero), jnp.where(lane >= A_HEAD_DIM, q, zero)], axis=0)

    qs = [stacked_queries(cols) for cols in head_cols]

    def rows(j):
        return pl.ds(pl.multiple_of(j * tk, tk), tk)

    def next_scores(h, j):
        return _dot_nt(k_ref[0, rows(j), head_cols[h]], qs[h])

    def values_product(h, j, p):
        return _dot(v_ref[0, h, j], p)

    def softmax_step(j, m, s, masked):
        if masked:
            q_off = lax.broadcasted_iota(jnp.int32, (1, 2 * tq), 1) & (tq - 1)
            delta = (i * tq - j * tk) // CHUNK
            s = jnp.concatenate([jnp.where(q_off >= (a - delta) * CHUNK, s[a * CHUNK:(a + 1) * CHUNK], NEG_INF)
                                 for a in range(tk // CHUNK)], axis=0)
        m_new = jnp.maximum(m, jnp.max(s, axis=0, keepdims=True))
        return m_new, jnp.exp2(m - m_new), jnp.exp2(s - m_new).astype(BF16)

    def body(j, carries):
        slot = j & 1
        loaded = [(s_buf[h, slot], p_buf[h, 1 - slot]) for h in range(n_heads)]
        pvs = [values_product(h, jnp.maximum(j - 1, 0), loaded[h][1]) for h in range(n_heads)]
        s_next = [next_scores(h, j + 1) for h in range(n_heads)]
        stepped = [softmax_step(j, carries[h][0], loaded[h][0], False) for h in range(n_heads)]
        for h, (_, _, p) in enumerate(stepped):
            s_buf[h, 1 - slot] = s_next[h]
            p_buf[h, slot] = p
        return tuple((m_new, alpha * (carries[h][1] + pvs[h])) for h, (m_new, alpha, _) in enumerate(stepped))

    for h in range(n_heads):
        s_buf[h, 0] = next_scores(h, 0)
        p_buf[h, 1] = jnp.zeros(p_buf.shape[2:], BF16)
    init = (jnp.full((1, 2 * tq), NEG_INF, F32), jnp.zeros((v_ref.shape[3], 2 * tq), F32))
    n_full = (i * tq) // tk
    carries = lax.fori_loop(0, n_full, body, (init,) * n_heads)

    lam_rows = lam_ref[...]
    lam = (jnp.exp(jnp.sum(lam_rows[0:1] * lam_rows[1:2], -1, keepdims=True))
           - jnp.exp(jnp.sum(lam_rows[2:3] * lam_rows[3:4], -1, keepdims=True)) + lam_init)
    last = n_full & 1
    for h in range(n_heads):
        pv = values_product(h, jnp.maximum(n_full - 1, 0), p_buf[h, 1 - last])
        _, alpha, p = softmax_step(n_full, carries[h][0], s_buf[h, last], True)
        acc = alpha * (carries[h][1] + pv) + values_product(h, n_full, p)
        out = acc[:A_VAL_DIM] / acc[A_VAL_DIM:A_VAL_DIM + 1]
        o = out[:, :tq] - lam * out[:, tq:]
        o = o * lax.rsqrt(jnp.mean(o * o, axis=0, keepdims=True) + RMS_EPS) * g_ref[...]
        o_ref[0, :, head_cols[h]] = (o * (1.0 - lam_init)).T.astype(o_ref.dtype)


def _diff_attn(qk, vt, lam_rows, subln_g, lam_init):
    b, _, n_ktiles, vt_rows, tk = vt.shape
    s = n_ktiles * tk
    tq = _tile(s, ATTN_Q_TILE)
    nh = ATTN_HEADS_PER_STEP
    width = nh * A_VAL_DIM
    k_blk0 = A_HEADS // nh
    return pl.pallas_call(
        functools.partial(_diff_attn_kernel, lam_init=lam_init),
        grid=(b, A_HEADS // nh, s // tq),
        in_specs=[
            pl.BlockSpec((4, A_HEAD_DIM), lambda bi, h, i: (0, 0)),
            pl.BlockSpec((A_VAL_DIM, 1), lambda bi, h, i: (0, 0)),
            pl.BlockSpec((1, tq, width), lambda bi, h, i: (bi, i, h)),
            pl.BlockSpec((1, s, width), lambda bi, h, i: (bi, 0, k_blk0 + h)),
            pl.BlockSpec((1, nh, n_ktiles, vt_rows, tk), lambda bi, h, i: (bi, h, 0, 0, 0)),
        ],
        out_specs=pl.BlockSpec((1, tq, width), lambda bi, h, i: (bi, i, h)),
        out_shape=jax.ShapeDtypeStruct((b, s, A_HEADS * A_VAL_DIM), BF16),
        scratch_shapes=[pltpu.VMEM((nh, 2, tk, 2 * tq), F32), pltpu.VMEM((nh, 2, tk, 2 * tq), BF16)],
        compiler_params=_params("parallel", "parallel", "arbitrary"),
        name="diff_attn",
    )(lam_rows, subln_g.reshape(A_VAL_DIM, 1), qk, qk, vt)


def _pool_kernel(p_ref, w_ref, scale_ref, o_ref):
    g = pl.program_id(1)
    x = p_ref[0].astype(F32)
    row = lax.broadcasted_iota(jnp.int32, x.shape, 0)
    for gi, window in enumerate(POOL_WINDOWS):
        @pl.when(g == gi)
        def _(window=window):
            ws = x
            shift = 1
            while shift < window:
                ws = ws + jnp.where(row >= shift, pltpu.roll(ws, shift, 0), 0.0)
                shift *= 2
            count = jnp.minimum(row + 1, window).astype(F32)
            d = ws / count - x
            y = _dot(d.astype(BF16), w_ref[0]) * scale_ref[...]
            o_ref[0] = y.astype(o_ref.dtype)


def _pool(u_pool, pool_w16, pool_scale):
    b, s, d = u_pool.shape
    cg = POOL_GROUP_DIM
    return pl.pallas_call(
        _pool_kernel,
        grid=(b, len(POOL_WINDOWS)),
        in_specs=[
            pl.BlockSpec((1, s, cg), lambda bi, g: (bi, 0, g)),
            pl.BlockSpec((1, cg, cg), lambda bi, g: (g, 0, 0)),
            pl.BlockSpec((1, cg), lambda bi, g: (0, g)),
        ],
        out_specs=pl.BlockSpec((1, s, cg), lambda bi, g: (bi, 0, g)),
        out_shape=jax.ShapeDtypeStruct((b, s, d), BF16),
        compiler_params=_params("parallel", "parallel"),
        name="pool_mixer",
    )(u_pool, pool_w16, pool_scale.reshape(1, d))


def _block_mask():
    r = lax.broadcasted_iota(jnp.int32, (RW_QUAD, RW_QUAD), 0)
    c = lax.broadcasted_iota(jnp.int32, (RW_QUAD, RW_QUAD), 1)
    return (r // R_HEAD_DIM) == (c // R_HEAD_DIM)


def _bd(x, mask):
    tiled = jnp.concatenate([x] * (RW_QUAD // R_HEAD_DIM), axis=0)
    return jnp.where(mask, tiled, jnp.zeros_like(tiled))


def _seg_quad(x, ones_bd):
    hi, lo = _split(x)
    return _dot(hi, ones_bd) + _dot(lo, ones_bd)


def _seg_sum(x, ones_bd):
    parts = [_seg_quad(x[:, q * RW_QUAD:(q + 1) * RW_QUAD], ones_bd) for q in range(x.shape[1] // RW_QUAD)]
    return jnp.concatenate(parts, axis=1)


def _chunk_summaries(units, mask):
    c = RW_CHUNK
    ii = lax.broadcasted_iota(jnp.int32, (c, RW_QUAD), 0)
    jj = lax.broadcasted_iota(jnp.int32, (c, RW_QUAD), 1) % c
    strict = ii > jj
    incl = ii >= jj
    eye = (ii == jj).astype(F32)
    bd = lambda x: _bd(x, mask)

    scaled = []
    for r, k, v, a, b, lw, cum in units:
        cum_end = cum[c - 1:c, :]
        at = a * jnp.exp(cum - lw)
        rt = r * jnp.exp(cum)
        e_neg = jnp.exp(-cum)
        w_end = jnp.exp(cum_end)
        bt = b * e_neg
        kt = k * e_neg
        scaled.append((at, rt, bt, kt, bt * w_end, kt * w_end, v, w_end))

    scores = [_mm(jnp.concatenate([at, rt], axis=0), jnp.concatenate([bd(bt), bd(kt)], axis=0), "nt")
              for at, rt, bt, kt, _, _, _, _ in scaled]
    a_ab = [jnp.where(strict, sc[:c, :RW_QUAD], 0.0) for sc in scores]
    a_rb = [jnp.where(incl, sc[c:, :RW_QUAD], 0.0) for sc in scores]
    av = [_mm(jnp.concatenate([jnp.where(strict, sc[:c, RW_QUAD:], 0.0), jnp.where(incl, sc[c:, RW_QUAD:], 0.0)],
                              axis=0), bd(u[6])) for sc, u in zip(scores, scaled)]

    tinv = [a + eye for a in a_ab]
    pk = [_mm(a, bd(a)) for a in a_ab]
    for _ in range(int(math.log2(c)) - 2):
        both = [_mm(jnp.concatenate([p, t], axis=0), bd(p)) for p, t in zip(pk, tinv)]
        tinv = [t + b[c:] for t, b in zip(tinv, both)]
        pk = [b[:c] for b in both]
    tinv = [t + _mm(t, bd(p)) for t, p in zip(tinv, pk)]

    side = lambda x, y: jnp.concatenate([x, y], axis=1)
    out = []
    for t, arb, avu, (at, rt, _, _, bh, kh, v, w_end) in zip(tinv, a_rb, av, scaled):
        x = _mm(t, side(bd(at), bd(avu[:c])))
        xa, xu = x[:, :RW_QUAD], x[:, RW_QUAD:]
        ry = _mm(arb, side(bd(xa), bd(xu)))
        z = _mm(bh, x, "tn")
        z1 = jnp.where(mask, z[:, :RW_QUAD], 0.0)
        z2 = jnp.where(mask, z[:, RW_QUAD:] + _mm(kh, v, "tn"), 0.0)
        fold = lambda z: functools.reduce(jnp.add, [z[h * c:(h + 1) * c] for h in range(RW_QUAD // R_HEAD_DIM)])
        out.append((rt + ry[:, :RW_QUAD], avu[c:] + ry[:, RW_QUAD:], fold(z1) + eye * w_end, fold(z2)))
    return out


def _rwkv_kernel(rkv_ref, lora_ref, vec_ref, w2_ref, a2_ref, g2_ref, o_ref, state_ref):
    ts = rkv_ref.shape[1]
    d = D_MODEL

    @pl.when(pl.program_id(1) == 0)
    def _():
        state_ref[...] = jnp.zeros_like(state_ref)

    xs = rkv_ref[0].astype(F32)
    ls = lora_ref[0].astype(F32)
    r = xs[:, :d]
    k = xs[:, d:2 * d]
    v = xs[:, 2 * d:]
    vec = vec_ref[...]
    w0, a0, k_k, k_a, r_k = vec[0:1], vec[1:2], vec[2:3], vec[3:4], vec[4:5]

    z = w0 + _dot(jnp.tanh(ls[:, :128]).astype(BF16), w2_ref[...])
    lw = -math.exp(-0.5) * jax.nn.sigmoid(z)
    a_sig = jax.nn.sigmoid(a0 + _dot(ls[:, 128:256].astype(BF16), a2_ref[...]))
    gate = _dot(jax.nn.sigmoid(ls[:, 256:]).astype(BF16), g2_ref[...])

    mask = _block_mask()
    ones_bd = mask.astype(BF16)
    kk = k * k_k
    kk = kk * lax.rsqrt(jnp.maximum(_seg_sum(kk * kk, ones_bd), KK_NORM_EPS * KK_NORM_EPS))
    k2 = k * (1.0 + (a_sig - 1.0) * k_a)
    a_vec = -kk
    b_vec = kk * a_sig
    bonus = _seg_sum(r * k2 * r_k, ones_bd) * v

    ti = lax.broadcasted_iota(jnp.int32, (ts, ts), 0)
    tj = lax.broadcasted_iota(jnp.int32, (ts, ts), 1)
    tri = ((tj <= ti) & (tj // RW_CHUNK == ti // RW_CHUNK)).astype(BF16)
    lw_hi, lw_lo = _split(lw)
    cum = _dot(tri, lw_hi) + _dot(tri, lw_lo)

    windows = [(slice(c * RW_CHUNK, (c + 1) * RW_CHUNK), slice(q * RW_QUAD, (q + 1) * RW_QUAD))
               for c in range(ts // RW_CHUNK) for q in range(d // RW_QUAD)]
    units = [tuple(x[rows, cols] for x in (r, k2, v, a_vec, b_vec, lw, cum)) for rows, cols in windows]
    summaries = _chunk_summaries(units, mask)

    inv_n = 1.0 / R_HEAD_DIM
    lnx_g, lnx_b = vec[5:6], vec[6:7]
    n_quads = d // RW_QUAD
    states = [state_ref[:, q * RW_QUAD:(q + 1) * RW_QUAD] for q in range(n_quads)]
    start_states = []
    for (rows, cols), (rp, y0, m, n) in zip(windows, summaries):
        q = cols.start // RW_QUAD
        start_bd = _bd(states[q], mask)
        start_states.append(start_bd)
        states[q] = _mm(m, start_bd) + n
    for q in range(n_quads):
        state_ref[:, q * RW_QUAD:(q + 1) * RW_QUAD] = states[q]

    ys = [_mm(rp, h0) + y0 for (rp, y0, _, _), h0 in zip(summaries, start_states)]
    mus = [_seg_quad(y, ones_bd) * inv_n for y in ys]
    dys = [y - mu for y, mu in zip(ys, mus)]
    variances = [_seg_quad(dy * dy, ones_bd) * inv_n for dy in dys]
    for (rows, cols), dy, var in zip(windows, dys, variances):
        yn = dy * lax.rsqrt(var + RWKV_GN_EPS) * lnx_g[:, cols] + lnx_b[:, cols]
        o_ref[0, rows, cols] = ((yn + bonus[rows, cols]) * gate[rows, cols]).astype(o_ref.dtype)


def _rwkv(u_rkv, u_lora, vecs, w2p, a2p, g2p):
    b, s, w3 = u_rkv.shape
    d = D_MODEL
    ts = _tile(s, RW_TILE)

    def tile_spec(width):
        return pl.BlockSpec((1, ts, width), lambda bi, t: (bi, t, 0))

    def const_spec(shape):
        return pl.BlockSpec(shape, lambda bi, t: (0,) * len(shape))

    return pl.pallas_call(
        _rwkv_kernel,
        grid=(b, s // ts),
        in_specs=[tile_spec(w3), tile_spec(LORA_PAD), const_spec(vecs.shape),
                  const_spec(w2p.shape), const_spec(a2p.shape), const_spec(g2p.shape)],
        out_specs=tile_spec(d),
        out_shape=jax.ShapeDtypeStruct((b, s, d), BF16),
        scratch_shapes=[pltpu.VMEM((R_HEAD_DIM, d), F32)],
        compiler_params=_params("parallel", "arbitrary"),
        name="rwkv7",
    )(u_rkv, u_lora, vecs, w2p, a2p, g2p)


def _merge_kernel(ya_ref, yb_ref, yc_ref, gates_ref, h_ref, wa_ref, wb_ref, wc_ref, wo_ref,
                  g_ref, b_ref, h32_ref, h16_ref):
    d = D_MODEL
    merged = jnp.zeros((ya_ref.shape[0], d), F32)
    for idx, (y_ref, w_ref) in enumerate(((ya_ref, wa_ref), (yb_ref, wb_ref), (yc_ref, wc_ref))):
        gate = gates_ref[:, idx * d:(idx + 1) * d].astype(F32)
        merged = merged + gate * _dot(y_ref[...], w_ref[...])
    z = DN_ALPHA * h_ref[...] + _dot(merged.astype(BF16), wo_ref[...])
    h = _ln(z, g_ref[...], b_ref[...], LN_EPS)
    h32_ref[...] = h
    h16_ref[...] = h.astype(BF16)


def _merge(ya, yb, yc, gates, h32, wa, wb, wc, wo, ln_g, ln_b):
    t, d = h32.shape
    tm = _tile(t, FUSED_ROW_TILE)
    row = pl.BlockSpec((tm, d), lambda i: (i, 0))
    wspec = pl.BlockSpec((d, d), lambda i: (0, 0))
    vec = pl.BlockSpec((1, d), lambda i: (0, 0))
    return pl.pallas_call(
        _merge_kernel,
        grid=(t // tm,),
        in_specs=[row, row, row, pl.BlockSpec((tm, 3 * d), lambda i: (i, 0)),
                  row, wspec, wspec, wspec, wspec, vec, vec],
        out_specs=[row, row],
        out_shape=[jax.ShapeDtypeStruct((t, d), F32), jax.ShapeDtypeStruct((t, d), BF16)],
        compiler_params=_params("parallel"),
        name="merge_out",
    )(ya, yb, yc, gates, h32, wa, wb, wc, wo, ln_g.reshape(1, d), ln_b.reshape(1, d))


def _cross_kernel(h16_ref, h32_ref, kv_ref, wq_ref, wo_ref, g_ref, b_ref, o32_ref, o16_ref):
    d = D_MODEL
    q = _dot(h16_ref[0], wq_ref[...]).astype(BF16)
    kv = kv_ref[0]
    head_cols = [slice(hd * X_HEAD_DIM, (hd + 1) * X_HEAD_DIM) for hd in range(X_HEADS)]
    scores = [_dot_nt(q[:, cols], kv[:, cols]) * (X_HEAD_DIM ** -0.5) for cols in head_cols]
    probs = [jnp.exp(s - jnp.max(s, -1, keepdims=True)) for s in scores]
    outs = [_dot(p.astype(BF16), kv[:, d + cols.start:d + cols.stop]) for p, cols in zip(probs, head_cols)]
    attn = jnp.concatenate([(o / jnp.sum(p, -1, keepdims=True)).astype(BF16) for o, p in zip(outs, probs)], axis=1)
    z = DN_ALPHA * h32_ref[0] + _dot(attn, wo_ref[...])
    h = _ln(z, g_ref[...], b_ref[...], LN_EPS)
    o32_ref[0] = h
    o16_ref[0] = h.astype(BF16)


def _cross(h16, h32, kv, wq, wo, ln_g, ln_b):
    b, s, d = h32.shape
    tm = _tile(s, FUSED_ROW_TILE)
    row = pl.BlockSpec((1, tm, d), lambda bi, i: (bi, i, 0))
    wspec = pl.BlockSpec((d, d), lambda bi, i: (0, 0))
    vec = pl.BlockSpec((1, d), lambda bi, i: (0, 0))
    n_mem = kv.shape[1]
    return pl.pallas_call(
        _cross_kernel,
        grid=(b, s // tm),
        in_specs=[row, row, pl.BlockSpec((1, n_mem, 2 * d), lambda bi, i: (bi, 0, 0)), wspec, wspec, vec, vec],
        out_specs=[row, row],
        out_shape=[jax.ShapeDtypeStruct((b, s, d), F32), jax.ShapeDtypeStruct((b, s, d), BF16)],
        compiler_params=_params("parallel", "parallel"),
        name="cross_attn",
    )(h16, h32, kv, wq, wo, ln_g.reshape(1, d), ln_b.reshape(1, d))


def _mlp_kernel(h16_ref, h32_ref, w1_ref, w2_ref, g_ref, b_ref, o32_ref, o16_ref):
    x = h16_ref[...]
    dff = w1_ref.shape[1]
    tf = min(MLP_FF_TILE, dff)
    acc = DN_ALPHA * h32_ref[...]
    for f in range(dff // tf):
        a = jnp.maximum(_dot(x, w1_ref[:, f * tf:(f + 1) * tf]), 0.0)
        acc = acc + _dot((a * a).astype(BF16), w2_ref[f * tf:(f + 1) * tf, :])
    h = _ln(acc, g_ref[...], b_ref[...], LN_EPS)
    o32_ref[...] = h
    o16_ref[...] = h.astype(BF16)


def _mlp(h16, h32, w1, w2, ln_g, ln_b):
    t, d = h32.shape
    dff = w1.shape[1]
    tm = _tile(t, MLP_ROW_TILE)
    row = pl.BlockSpec((tm, d), lambda i: (i, 0))
    vec = pl.BlockSpec((1, d), lambda i: (0, 0))
    resident = pl.Buffered(1)
    return pl.pallas_call(
        _mlp_kernel,
        grid=(t // tm,),
        in_specs=[row, row, pl.BlockSpec((d, dff), lambda i: (0, 0), pipeline_mode=resident),
                  pl.BlockSpec((dff, d), lambda i: (0, 0), pipeline_mode=resident), vec, vec],
        out_specs=[row, row],
        out_shape=[jax.ShapeDtypeStruct((t, d), F32), jax.ShapeDtypeStruct((t, d), BF16)],
        compiler_params=_params("parallel"),
        name="mlp",
    )(h16, h32, w1, w2, ln_g.reshape(1, d), ln_b.reshape(1, d))


def _pad_rows(w, rows):
    return jnp.pad(w, ((0, rows - w.shape[0]), (0, 0)))


def _lora_layout(w):
    pad = lambda t, n: jnp.pad(t, [(0, 0)] * (t.ndim - 1) + [(0, n - t.shape[-1])])
    xw = w[..., :DECAY_LORA]
    xa = w[..., DECAY_LORA:DECAY_LORA + AAA_LORA]
    xg = w[..., DECAY_LORA + AAA_LORA:]
    return jnp.concatenate([pad(xw, 128), pad(xa, 128), pad(xg, 256)], axis=-1)


def kernel(x, mem, positions, ln_in_g, ln_in_b, w_in, b_gate, lam_q1, lam_k1, lam_q2, lam_k2, attn_subln_g, w_br_attn, pool_w, pool_scale, w_br_pool, rwkv_mu, rwkv_w0, rwkv_w2, rwkv_a0, rwkv_a2, rwkv_g2, rwkv_k_k, rwkv_k_a, rwkv_r_k, rwkv_lnx_g, rwkv_lnx_b, w_br_rwkv, w_out, ln1_g, ln1_b, w_xq, w_xkv, w_xo, ln2_g, ln2_b, w_ff1, w_ff2, ln3_g, ln3_b):
    bsz, seq, d = x.shape
    t = bsz * seq
    n_mem = mem.shape[1]
    depth = w_in.shape[0]

    inv_freq = 1.0 / (ROPE_THETA ** (jnp.arange(0, A_HEAD_DIM, 2, dtype=F32) / A_HEAD_DIM))
    ang = positions.astype(F32).reshape(t, 1) * inv_freq
    cos, sin = jnp.cos(ang), jnp.sin(ang)
    cos_t = jnp.concatenate([cos, cos, cos, cos], axis=-1)
    sin_t = jnp.concatenate([-sin, sin, -sin, sin], axis=-1)

    c_qk = 2 * A_HEADS * 2 * A_HEAD_DIM
    c_v = c_qk + A_HEADS * A_VAL_DIM
    c_pool = c_v + d
    c_rkv = c_pool + 3 * d
    c_lora = c_rkv + DECAY_LORA + AAA_LORA + GATE_LORA

    h32, h16 = _ln_in(x.reshape(t, d), ln_in_g, ln_in_b)
    mem2d = mem.reshape(bsz * n_mem, d)

    for l in range(depth):
        w = w_in[l]
        weights = (w[:, :c_rkv].astype(BF16), _lora_layout(w[:, c_rkv:c_lora]).astype(BF16), w[:, c_lora:].astype(BF16))
        widths = (c_qk, c_v - c_qk, c_pool - c_v, c_rkv - c_pool, LORA_PAD, w.shape[1] - c_lora)
        mu = rwkv_mu[l]
        shift_mu = jnp.concatenate([mu[:3 * d], _lora_layout(mu[3 * d:])]).reshape(1, 3 * d + LORA_PAD)
        qk, vt, u_pool, u_rkv, u_lora, gates = _in_proj(h16, weights, cos_t, sin_t, b_gate[l].reshape(1, 3 * d), shift_mu,
                                                        widths, seq, min(ATTN_K_TILE, seq))
        qk = qk.reshape(bsz, seq, c_qk)
        u_pool = u_pool.reshape(bsz, seq, d)
        u_rkv = u_rkv.reshape(bsz, seq, 3 * d)
        u_lora = u_lora.reshape(bsz, seq, LORA_PAD)

        lam_init = 0.8 - 0.6 * math.exp(-0.3 * l)
        lam_rows = jnp.stack([lam_q1[l], lam_k1[l], lam_q2[l], lam_k2[l]]).astype(F32)
        y_a = _diff_attn(qk, vt, lam_rows, attn_subln_g[l], lam_init)

        y_b = _pool(u_pool, pool_w[l].astype(BF16), pool_scale[l])

        vecs = jnp.stack([rwkv_w0[l], rwkv_a0[l], rwkv_k_k[l], rwkv_k_a[l], rwkv_r_k[l].reshape(d),
                          rwkv_lnx_g[l], rwkv_lnx_b[l], jnp.zeros((d,), F32)])
        w2p = _pad_rows(rwkv_w2[l], 128).astype(BF16)
        a2p = _pad_rows(rwkv_a2[l], 128).astype(BF16)
        g2p = _pad_rows(rwkv_g2[l], 256).astype(BF16)
        y_c = _rwkv(u_rkv, u_lora, vecs, w2p, a2p, g2p)

        h32, h16 = _merge(y_a.reshape(t, d), y_b.reshape(t, d), y_c.reshape(t, d), gates, h32,
                          w_br_attn[l].astype(BF16), w_br_pool[l].astype(BF16), w_br_rwkv[l].astype(BF16),
                          w_out[l].astype(BF16), ln1_g[l], ln1_b[l])

        kv = _proj(mem2d, w_xkv[l].astype(BF16), BF16, name="proj_kv").reshape(bsz, n_mem, 2 * d)
        h32, h16 = _cross(h16.reshape(bsz, seq, d), h32.reshape(bsz, seq, d), kv, w_xq[l].astype(BF16),
                          w_xo[l].astype(BF16), ln2_g[l], ln2_b[l])
        h32, h16 = h32.reshape(t, d), h16.reshape(t, d)

        h32, h16 = _mlp(h16, h32, w_ff1[l].astype(BF16), w_ff2[l].astype(BF16), ln3_g[l], ln3_b[l])

    return h32.reshape(bsz, seq, d)
```

```python
import functools
import math

import jax
import jax.numpy as jnp
from jax import lax
from jax.experimental import pallas as pl
from jax.experimental.pallas import tpu as pltpu

F32 = jnp.float32
BF16 = jnp.bfloat16

D_MODEL = 1024
DEPTH = 2
CHUNK = 64
A_HEADS = 8
A_HEAD_DIM = 64
A_VAL_DIM = 2 * A_HEAD_DIM
ROPE_THETA = 10000.0
POOL_WINDOWS = (2, 4, 8, 16)
POOL_GROUP_DIM = D_MODEL // len(POOL_WINDOWS)
R_HEAD_DIM = 64
DECAY_LORA = 64
AAA_LORA = 64
GATE_LORA = 160
X_HEADS = 4
X_HEAD_DIM = D_MODEL // X_HEADS
D_FF = 4 * D_MODEL
DN_ALPHA = (2 * DEPTH) ** 0.25
LN_EPS = 1e-5
RMS_EPS = 1e-5
RWKV_GN_EPS = 64e-5
KK_NORM_EPS = 1e-12
NEG_INF = -1e30
LOG2_E = math.log2(math.e)
ONES_ROWS = 16

V7X_LANES = 128
V7X_SUBLANES = 8
V7X_MXU_DIM = 256
V7X_VMEM_LIMIT_BYTES = 56 * 1024 * 1024

ROW_TILE = 1024
COL_TILE = 1024
IN_PROJ_ROW_TILE = 512
IN_PROJ_CHUNK = 1024
FUSED_ROW_TILE = 512
MLP_ROW_TILE = 512
MLP_FF_TILE = 1024
ATTN_Q_TILE = 256
ATTN_K_TILE = 256
ATTN_HEADS_PER_STEP = 2
RW_CHUNK = 64
RW_QUAD = V7X_MXU_DIM
RW_TILE = 256
HALO_ROWS = 16
LORA_PAD = 512


def _tile(n, pref):
    return pref if n % pref == 0 else n


def _params(*sem):
    return pltpu.CompilerParams(dimension_semantics=sem, vmem_limit_bytes=V7X_VMEM_LIMIT_BYTES)


def _ln(z, g, b, eps):
    mu = jnp.mean(z, -1, keepdims=True)
    d = z - mu
    var = jnp.mean(d * d, -1, keepdims=True)
    return d * lax.rsqrt(var + eps) * g + b


_DIMS = {"nn": (((1,), (0,)), ((), ())), "nt": (((1,), (1,)), ((), ())), "tn": (((0,), (0,)), ((), ()))}


def _dot(a, b):
    return jnp.dot(a, b, preferred_element_type=F32)


def _dot_nt(a, b):
    return lax.dot_general(a, b, _DIMS["nt"], preferred_element_type=F32)


def _sigmoid(x):
    return 0.5 * jnp.tanh(0.5 * x) + 0.5


def _split(x):
    hi = x.astype(BF16)
    return hi, (x - hi.astype(F32)).astype(BF16)


def _mm(a, b, kind="nn"):
    return lax.dot_general(a.astype(BF16), b.astype(BF16), _DIMS[kind], preferred_element_type=F32)


def _ln_in_kernel(x_ref, g_ref, b_ref, h32_ref, h16_ref):
    h = _ln(x_ref[...], g_ref[...], b_ref[...], LN_EPS)
    h32_ref[...] = h
    h16_ref[...] = h.astype(BF16)


def _ln_in(x2d, g, b):
    t, d = x2d.shape
    tm = _tile(t, ROW_TILE)
    row = pl.BlockSpec((tm, d), lambda i: (i, 0))
    vec = pl.BlockSpec((1, d), lambda i: (0, 0))
    return pl.pallas_call(
        _ln_in_kernel,
        grid=(t // tm,),
        in_specs=[row, vec, vec],
        out_specs=[row, row],
        out_shape=[jax.ShapeDtypeStruct((t, d), F32), jax.ShapeDtypeStruct((t, d), BF16)],
        compiler_params=_params("parallel"),
        name="ln_in",
    )(x2d, g.reshape(1, d), b.reshape(1, d))


def _proj_kernel(x_ref, w_ref, o_ref):
    o_ref[...] = _dot(x_ref[...].astype(BF16), w_ref[...]).astype(o_ref.dtype)


def _proj(x, w16, out_dtype, name):
    m, k = x.shape
    n = w16.shape[1]
    tm = _tile(m, ROW_TILE)
    tn = _tile(n, COL_TILE)
    return pl.pallas_call(
        _proj_kernel,
        grid=(m // tm, n // tn),
        in_specs=[pl.BlockSpec((tm, k), lambda i, j: (i, 0)), pl.BlockSpec((k, tn), lambda i, j: (0, j))],
        out_specs=pl.BlockSpec((tm, tn), lambda i, j: (i, j)),
        out_shape=jax.ShapeDtypeStruct((m, n), out_dtype),
        compiler_params=_params("parallel", "parallel"),
        name=name,
    )(x, w16)


def _rope(t, cos, sin, first_half):
    partner = jnp.where(first_half, pltpu.roll(t, V7X_LANES - A_HEAD_DIM // 2, 1), pltpu.roll(t, A_HEAD_DIM // 2, 1))
    return t * cos + partner * sin


def _in_proj_kernel(x_ref, halo_ref, w_ref, cos_ref, sin_ref, bias_ref, mu_ref,
                    qk_ref, v_ref, pool_ref, rkv_ref, lora_ref, gate_ref, *, tiles_per_seq):
    x = x_ref[...]
    at_start = pl.program_id(0) % tiles_per_seq == 0
    halo = halo_ref[...]
    x_ext = jnp.concatenate([jnp.where(at_start, jnp.zeros_like(halo), halo), x], axis=0)
    cos = cos_ref[...]
    sin = sin_ref[...]
    lane = lax.broadcasted_iota(jnp.int32, cos.shape, 1)
    first_half = (lane & (A_HEAD_DIM - 1)) < A_HEAD_DIM // 2
    col = 0
    mu_col = 0
    for o_ref in (qk_ref, v_ref, pool_ref, rkv_ref, lora_ref, gate_ref):
        width = A_HEADS * A_VAL_DIM if o_ref is v_ref else o_ref.shape[1]
        for c0 in range(0, width, IN_PROJ_CHUNK):
            c1 = min(c0 + IN_PROJ_CHUNK, width)
            if o_ref is rkv_ref or o_ref is lora_ref:
                t_ext = _dot(x_ext, w_ref[:, col + c0:col + c1])
                t = t_ext[HALO_ROWS:]
                prev = pltpu.roll(t_ext, 1, 0)[HALO_ROWS:]
                mu = mu_ref[:, mu_col + c0:mu_col + c1]
                o_ref[:, c0:c1] = (t + (prev - t) * mu).astype(o_ref.dtype)
                continue
            t = _dot(x, w_ref[:, col + c0:col + c1])
            if o_ref is qk_ref:
                for j in range(c0, c1, V7X_LANES):
                    blk = t[:, j - c0:j - c0 + V7X_LANES]
                    o_ref[:, j:j + V7X_LANES] = _rope(blk, cos, sin, first_half).astype(o_ref.dtype)
            elif o_ref is gate_ref:
                o_ref[:, c0:c1] = jax.nn.sigmoid(t + bias_ref[:, c0:c1]).astype(o_ref.dtype)
            elif o_ref is v_ref:
                tk = v_ref.shape[4]
                for h in range((c1 - c0) // A_VAL_DIM):
                    for kt in range(v_ref.shape[2]):
                        blk = t[kt * tk:(kt + 1) * tk, h * A_VAL_DIM:(h + 1) * A_VAL_DIM]
                        head = c0 // A_VAL_DIM + h
                        v_ref[0, head, kt, :A_VAL_DIM, :] = blk.T.astype(v_ref.dtype)
                        v_ref[0, head, kt, A_VAL_DIM:, :] = jnp.ones((ONES_ROWS, tk), v_ref.dtype)
            else:
                o_ref[:, c0:c1] = t.astype(o_ref.dtype)
        col += width
        if o_ref is rkv_ref or o_ref is lora_ref:
            mu_col += width


def _in_proj(x16, w16, cos_t, sin_t, gate_bias, shift_mu, widths, seq, tk):
    m, k = x16.shape
    tm = _tile(seq, IN_PROJ_ROW_TILE)
    tiles_per_seq = seq // tm
    row = lambda n: pl.BlockSpec((tm, n), lambda i: (i, 0))
    vt_rows = A_VAL_DIM + ONES_ROWS
    vt_spec = pl.BlockSpec((1, A_HEADS, tm // tk, vt_rows, tk),
                           lambda i: (i // tiles_per_seq, 0, i % tiles_per_seq, 0, 0))
    out_specs = [row(n) for n in widths]
    out_shape = [jax.ShapeDtypeStruct((m, n), BF16) for n in widths]
    out_specs[1] = vt_spec
    out_shape[1] = jax.ShapeDtypeStruct((m // seq, A_HEADS, seq // tk, vt_rows, tk), BF16)
    halo_blocks = tm // HALO_ROWS
    halo = pl.BlockSpec((HALO_ROWS, k), lambda i: (jnp.maximum(i * halo_blocks - 1, 0), 0))
    const = lambda a: pl.BlockSpec(a.shape, lambda i: (0, 0))
    return pl.pallas_call(
        functools.partial(_in_proj_kernel, tiles_per_seq=tiles_per_seq),
        grid=(m // tm,),
        in_specs=[row(k), halo, pl.BlockSpec(w16.shape, lambda i: (0, 0), pipeline_mode=pl.Buffered(1)),
                  row(V7X_LANES), row(V7X_LANES), const(gate_bias), const(shift_mu)],
        out_specs=out_specs,
        out_shape=out_shape,
        compiler_params=_params("parallel"),
        name="in_proj",
    )(x16, x16, w16, cos_t, sin_t, gate_bias, shift_mu)


def _diff_attn_kernel(lam_ref, g_ref, q_ref, k_ref, v_ref, o_ref, s_buf, p_buf, *, lam_init):
    i = pl.program_id(2)
    tq = q_ref.shape[1]
    tk = s_buf.shape[2]
    n_heads = s_buf.shape[0]
    head_cols = [slice(h * A_VAL_DIM, (h + 1) * A_VAL_DIM) for h in range(n_heads)]

    def stacked_queries(cols):
        q = (q_ref[0, :, cols].astype(F32) * (A_HEAD_DIM ** -0.5 * LOG2_E)).astype(BF16)
        lane = lax.broadcasted_iota(jnp.int32, q.shape, 1)
        zero = jnp.zeros_like(q)
        return jnp.concatenate([jnp.where(lane < A_HEAD_DIM, q, zero), jnp.where(lane >= A_HEAD_DIM, q, zero)], axis=0)

    qs = [stacked_queries(cols) for cols in head_cols]

    def rows(j):
        return pl.ds(pl.multiple_of(j * tk, tk), tk)

    def next_scores(h, j):
        return _dot_nt(k_ref[0, rows(j), head_cols[h]], qs[h])

    def values_product(h, j, p):
        return _dot(v_ref[0, h, j], p)

    def softmax_step(j, m, s, masked):
        if masked:
            q_off = lax.broadcasted_iota(jnp.int32, (1, 2 * tq), 1) & (tq - 1)
            delta = (i * tq - j * tk) // CHUNK
            s = jnp.concatenate([jnp.where(q_off >= (a - delta) * CHUNK, s[a * CHUNK:(a + 1) * CHUNK], NEG_INF)
                                 for a in range(tk // CHUNK)], axis=0)
        m_new = jnp.maximum(m, jnp.max(s, axis=0, keepdims=True))
        return m_new, jnp.exp2(m - m_new), jnp.exp2(s - m_new).astype(BF16)

    def body(j, carries):
        slot = j & 1
        loaded = [(s_buf[h, slot], p_buf[h, 1 - slot]) for h in range(n_heads)]
        pvs = [values_product(h, jnp.maximum(j - 1, 0), loaded[h][1]) for h in range(n_heads)]
        s_next = [next_scores(h, j + 1) for h in range(n_heads)]
        stepped = [softmax_step(j, carries[h][0], loaded[h][0], False) for h in range(n_heads)]
        for h, (_, _, p) in enumerate(stepped):
            s_buf[h, 1 - slot] = s_next[h]
            p_buf[h, slot] = p
        return tuple((m_new, alpha * (carries[h][1] + pvs[h])) for h, (m_new, alpha, _) in enumerate(stepped))

    for h in range(n_heads):
        s_buf[h, 0] = next_scores(h, 0)
        p_buf[h, 1] = jnp.zeros(p_buf.shape[2:], BF16)
    init = (jnp.full((1, 2 * tq), NEG_INF, F32), jnp.zeros((v_ref.shape[3], 2 * tq), F32))
    n_full = (i * tq) // tk
    carries = lax.fori_loop(0, n_full, body, (init,) * n_heads)

    lam_rows = lam_ref[...]
    lam = (jnp.exp(jnp.sum(lam_rows[0:1] * lam_rows[1:2], -1, keepdims=True))
           - jnp.exp(jnp.sum(lam_rows[2:3] * lam_rows[3:4], -1, keepdims=True)) + lam_init)
    last = n_full & 1
    for h in range(n_heads):
        pv = values_product(h, jnp.maximum(n_full - 1, 0), p_buf[h, 1 - last])
        _, alpha, p = softmax_step(n_full, carries[h][0], s_buf[h, last], True)
        acc = alpha * (carries[h][1] + pv) + values_product(h, n_full, p)
        out = acc[:A_VAL_DIM] / acc[A_VAL_DIM:A_VAL_DIM + 1]
        o = out[:, :tq] - lam * out[:, tq:]
        o = o * lax.rsqrt(jnp.mean(o * o, axis=0, keepdims=True) + RMS_EPS) * g_ref[...]
        o_ref[0, :, head_cols[h]] = (o * (1.0 - lam_init)).T.astype(o_ref.dtype)


def _diff_attn(qk, vt, lam_rows, subln_g, lam_init):
    b, _, n_ktiles, vt_rows, tk = vt.shape
    s = n_ktiles * tk
    tq = _tile(s, ATTN_Q_TILE)
    nh = ATTN_HEADS_PER_STEP
    width = nh * A_VAL_DIM
    k_blk0 = A_HEADS // nh
    return pl.pallas_call(
        functools.partial(_diff_attn_kernel, lam_init=lam_init),
        grid=(b, A_HEADS // nh, s // tq),
        in_specs=[
            pl.BlockSpec((4, A_HEAD_DIM), lambda bi, h, i: (0, 0)),
            pl.BlockSpec((A_VAL_DIM, 1), lambda bi, h, i: (0, 0)),
            pl.BlockSpec((1, tq, width), lambda bi, h, i: (bi, i, h)),
            pl.BlockSpec((1, s, width), lambda bi, h, i: (bi, 0, k_blk0 + h)),
            pl.BlockSpec((1, nh, n_ktiles, vt_rows, tk), lambda bi, h, i: (bi, h, 0, 0, 0)),
        ],
        out_specs=pl.BlockSpec((1, tq, width), lambda bi, h, i: (bi, i, h)),
        out_shape=jax.ShapeDtypeStruct((b, s, A_HEADS * A_VAL_DIM), BF16),
        scratch_shapes=[pltpu.VMEM((nh, 2, tk, 2 * tq), F32), pltpu.VMEM((nh, 2, tk, 2 * tq), BF16)],
        compiler_params=_params("parallel", "parallel", "arbitrary"),
        name="diff_attn",
    )(lam_rows, subln_g.reshape(A_VAL_DIM, 1), qk, qk, vt)


def _pool_kernel(p_ref, w_ref, scale_ref, o_ref):
    g = pl.program_id(1)
    x = p_ref[0].astype(F32)
    row = lax.broadcasted_iota(jnp.int32, x.shape, 0)
    for gi, window in enumerate(POOL_WINDOWS):
        @pl.when(g == gi)
        def _(window=window):
            ws = x
            shift = 1
            while shift < window:
                ws = ws + jnp.where(row >= shift, pltpu.roll(ws, shift, 0), 0.0)
                shift *= 2
            count = jnp.minimum(row + 1, window).astype(F32)
            d = ws / count - x
            y = _dot(d.astype(BF16), w_ref[0]) * scale_ref[...]
            o_ref[0] = y.astype(o_ref.dtype)


def _pool(u_pool, pool_w16, pool_scale):
    b, s, d = u_pool.shape
    cg = POOL_GROUP_DIM
    return pl.pallas_call(
        _pool_kernel,
        grid=(b, len(POOL_WINDOWS)),
        in_specs=[
            pl.BlockSpec((1, s, cg), lambda bi, g: (bi, 0, g)),
            pl.BlockSpec((1, cg, cg), lambda bi, g: (g, 0, 0)),
            pl.BlockSpec((1, cg), lambda bi, g: (0, g)),
        ],
        out_specs=pl.BlockSpec((1, s, cg), lambda bi, g: (bi, 0, g)),
        out_shape=jax.ShapeDtypeStruct((b, s, d), BF16),
        compiler_params=_params("parallel", "parallel"),
        name="pool_mixer",
    )(u_pool, pool_w16, pool_scale.reshape(1, d))


def _block_mask():
    r = lax.broadcasted_iota(jnp.int32, (RW_QUAD, RW_QUAD), 0)
    c = lax.broadcasted_iota(jnp.int32, (RW_QUAD, RW_QUAD), 1)
    return (r // R_HEAD_DIM) == (c // R_HEAD_DIM)


def _bd(x, mask):
    tiled = jnp.concatenate([x] * (RW_QUAD // R_HEAD_DIM), axis=0)
    return jnp.where(mask, tiled, jnp.zeros_like(tiled))


def _seg_quad(x, ones_bd):
    hi, lo = _split(x)
    return _dot(hi, ones_bd) + _dot(lo, ones_bd)


def _seg_sum(x, ones_bd):
    parts = [_seg_quad(x[:, q * RW_QUAD:(q + 1) * RW_QUAD], ones_bd) for q in range(x.shape[1] // RW_QUAD)]
    return jnp.concatenate(parts, axis=1)


def _chunk_summaries(units, mask):
    c = RW_CHUNK
    ii = lax.broadcasted_iota(jnp.int32, (c, RW_QUAD), 0)
    jj = lax.broadcasted_iota(jnp.int32, (c, RW_QUAD), 1) % c
    strict = ii > jj
    incl = ii >= jj
    eye = (ii == jj).astype(F32)
    bd = lambda x: _bd(x, mask)

    scaled = []
    for r, k, v, a, b, lw, cum in units:
        cum_end = cum[c - 1:c, :]
        at = a * jnp.exp(cum - lw)
        rt = r * jnp.exp(cum)
        e_neg = jnp.exp(-cum)
        w_end = jnp.exp(cum_end)
        bt = b * e_neg
        kt = k * e_neg
        scaled.append((at, rt, bt, kt, bt * w_end, kt * w_end, v, w_end))

    scores = [_mm(jnp.concatenate([at, rt], axis=0), jnp.concatenate([bd(bt), bd(kt)], axis=0), "nt")
              for at, rt, bt, kt, _, _, _, _ in scaled]
    a_ab = [jnp.where(strict, sc[:c, :RW_QUAD], 0.0) for sc in scores]
    a_rb = [jnp.where(incl, sc[c:, :RW_QUAD], 0.0) for sc in scores]
    av = [_mm(jnp.concatenate([jnp.where(strict, sc[:c, RW_QUAD:], 0.0), jnp.where(incl, sc[c:, RW_QUAD:], 0.0)],
                              axis=0), bd(u[6])) for sc, u in zip(scores, scaled)]

    tinv = [a + eye for a in a_ab]
    pk = [_mm(a, bd(a)) for a in a_ab]
    for _ in range(int(math.log2(c)) - 2):
        both = [_mm(jnp.concatenate([p, t], axis=0), bd(p)) for p, t in zip(pk, tinv)]
        tinv = [t + b[c:] for t, b in zip(tinv, both)]
        pk = [b[:c] for b in both]
    tinv = [t + _mm(t, bd(p)) for t, p in zip(tinv, pk)]

    side = lambda x, y: jnp.concatenate([x, y], axis=1)
    out = []
    for t, arb, avu, (at, rt, _, _, bh, kh, v, w_end) in zip(tinv, a_rb, av, scaled):
        x = _mm(t, side(bd(at), bd(avu[:c])))
        xa, xu = x[:, :RW_QUAD], x[:, RW_QUAD:]
        ry = _mm(arb, side(bd(xa), bd(xu)))
        z = _mm(bh, x, "tn")
        z1 = jnp.where(mask, z[:, :RW_QUAD], 0.0)
        z2 = jnp.where(mask, z[:, RW_QUAD:] + _mm(kh, v, "tn"), 0.0)
        fold = lambda z: functools.reduce(jnp.add, [z[h * c:(h + 1) * c] for h in range(RW_QUAD // R_HEAD_DIM)])
        out.append((rt + ry[:, :RW_QUAD], avu[c:] + ry[:, RW_QUAD:], fold(z1) + eye * w_end, fold(z2)))
    return out


def _rwkv_kernel(rkv_ref, lora_ref, vec_ref, w2_ref, a2_ref, g2_ref, o_ref, state_ref):
    ts = rkv_ref.shape[1]
    d = D_MODEL

    @pl.when(pl.program_id(1) == 0)
    def _():
        state_ref[...] = jnp.zeros_like(state_ref)

    xs = rkv_ref[0].astype(F32)
    ls = lora_ref[0].astype(F32)
    r = xs[:, :d]
    k = xs[:, d:2 * d]
    v = xs[:, 2 * d:]
    vec = vec_ref[...]
    w0, a0, k_k, k_a, r_k = vec[0:1], vec[1:2], vec[2:3], vec[3:4], vec[4:5]

    z = w0 + _dot(jnp.tanh(ls[:, :128]).astype(BF16), w2_ref[...])
    lw = -math.exp(-0.5) * _sigmoid(z)
    a_sig = _sigmoid(a0 + _dot(ls[:, 128:256].astype(BF16), a2_ref[...]))
    gate = _dot(_sigmoid(ls[:, 256:]).astype(BF16), g2_ref[...])

    mask = _block_mask()
    ones_bd = mask.astype(BF16)
    kk = k * k_k
    kk = kk * lax.rsqrt(jnp.maximum(_seg_sum(kk * kk, ones_bd), KK_NORM_EPS * KK_NORM_EPS))
    k2 = k * (1.0 + (a_sig - 1.0) * k_a)
    a_vec = -kk
    b_vec = kk * a_sig
    bonus = _seg_sum(r * k2 * r_k, ones_bd) * v

    ti = lax.broadcasted_iota(jnp.int32, (ts, ts), 0)
    tj = lax.broadcasted_iota(jnp.int32, (ts, ts), 1)
    tri = ((tj <= ti) & (tj // RW_CHUNK == ti // RW_CHUNK)).astype(BF16)
    lw_hi, lw_lo = _split(lw)
    cum = _dot(tri, lw_hi) + _dot(tri, lw_lo)

    windows = [(slice(c * RW_CHUNK, (c + 1) * RW_CHUNK), slice(q * RW_QUAD, (q + 1) * RW_QUAD))
               for c in range(ts // RW_CHUNK) for q in range(d // RW_QUAD)]
    units = [tuple(x[rows, cols] for x in (r, k2, v, a_vec, b_vec, lw, cum)) for rows, cols in windows]
    summaries = _chunk_summaries(units, mask)

    inv_n = 1.0 / R_HEAD_DIM
    lnx_g, lnx_b = vec[5:6], vec[6:7]
    n_quads = d // RW_QUAD
    states = [state_ref[:, q * RW_QUAD:(q + 1) * RW_QUAD] for q in range(n_quads)]
    start_states = []
    for (rows, cols), (rp, y0, m, n) in zip(windows, summaries):
        q = cols.start // RW_QUAD
        start_bd = _bd(states[q], mask)
        start_states.append(start_bd)
        states[q] = _mm(m, start_bd) + n
    for q in range(n_quads):
        state_ref[:, q * RW_QUAD:(q + 1) * RW_QUAD] = states[q]

    ys = [_mm(rp, h0) + y0 for (rp, y0, _, _), h0 in zip(summaries, start_states)]
    mus = [_seg_quad(y, ones_bd) * inv_n for y in ys]
    dys = [y - mu for y, mu in zip(ys, mus)]
    variances = [_seg_quad(dy * dy, ones_bd) * inv_n for dy in dys]
    for (rows, cols), dy, var in zip(windows, dys, variances):
        yn = dy * lax.rsqrt(var + RWKV_GN_EPS) * lnx_g[:, cols] + lnx_b[:, cols]
        o_ref[0, rows, cols] = ((yn + bonus[rows, cols]) * gate[rows, cols]).astype(o_ref.dtype)


def _rwkv(u_rkv, u_lora, vecs, w2p, a2p, g2p):
    b, s, w3 = u_rkv.shape
    d = D_MODEL
    ts = _tile(s, RW_TILE)

    def tile_spec(width):
        return pl.BlockSpec((1, ts, width), lambda bi, t: (bi, t, 0))

    def const_spec(shape):
        return pl.BlockSpec(shape, lambda bi, t: (0,) * len(shape))

    return pl.pallas_call(
        _rwkv_kernel,
        grid=(b, s // ts),
        in_specs=[tile_spec(w3), tile_spec(LORA_PAD), const_spec(vecs.shape),
                  const_spec(w2p.shape), const_spec(a2p.shape), const_spec(g2p.shape)],
        out_specs=tile_spec(d),
        out_shape=jax.ShapeDtypeStruct((b, s, d), BF16),
        scratch_shapes=[pltpu.VMEM((R_HEAD_DIM, d), F32)],
        compiler_params=_params("parallel", "arbitrary"),
        name="rwkv7",
    )(u_rkv, u_lora, vecs, w2p, a2p, g2p)


def _merge_kernel(ya_ref, yb_ref, yc_ref, gates_ref, h_ref, wa_ref, wb_ref, wc_ref, wo_ref,
                  g_ref, b_ref, h32_ref, h16_ref):
    d = D_MODEL
    merged = jnp.zeros((ya_ref.shape[0], d), F32)
    for idx, (y_ref, w_ref) in enumerate(((ya_ref, wa_ref), (yb_ref, wb_ref), (yc_ref, wc_ref))):
        gate = gates_ref[:, idx * d:(idx + 1) * d].astype(F32)
        merged = merged + gate * _dot(y_ref[...], w_ref[...])
    z = DN_ALPHA * h_ref[...] + _dot(merged.astype(BF16), wo_ref[...])
    h = _ln(z, g_ref[...], b_ref[...], LN_EPS)
    h32_ref[...] = h
    h16_ref[...] = h.astype(BF16)


def _merge(ya, yb, yc, gates, h32, wa, wb, wc, wo, ln_g, ln_b):
    t, d = h32.shape
    tm = _tile(t, FUSED_ROW_TILE)
    row = pl.BlockSpec((tm, d), lambda i: (i, 0))
    wspec = pl.BlockSpec((d, d), lambda i: (0, 0))
    vec = pl.BlockSpec((1, d), lambda i: (0, 0))
    return pl.pallas_call(
        _merge_kernel,
        grid=(t // tm,),
        in_specs=[row, row, row, pl.BlockSpec((tm, 3 * d), lambda i: (i, 0)),
                  row, wspec, wspec, wspec, wspec, vec, vec],
        out_specs=[row, row],
        out_shape=[jax.ShapeDtypeStruct((t, d), F32), jax.ShapeDtypeStruct((t, d), BF16)],
        compiler_params=_params("parallel"),
        name="merge_out",
    )(ya, yb, yc, gates, h32, wa, wb, wc, wo, ln_g.reshape(1, d), ln_b.reshape(1, d))


def _cross_kernel(h16_ref, h32_ref, kv_ref, wq_ref, wo_ref, g_ref, b_ref, o32_ref, o16_ref):
    d = D_MODEL
    q = _dot(h16_ref[0], wq_ref[...]).astype(BF16)
    kv = kv_ref[0]
    head_cols = [slice(hd * X_HEAD_DIM, (hd + 1) * X_HEAD_DIM) for hd in range(X_HEADS)]
    scores = [_dot_nt(q[:, cols], kv[:, cols]) * (X_HEAD_DIM ** -0.5) for cols in head_cols]
    probs = [jnp.exp(s - jnp.max(s, -1, keepdims=True)) for s in scores]
    outs = [_dot(p.astype(BF16), kv[:, d + cols.start:d + cols.stop]) for p, cols in zip(probs, head_cols)]
    attn = jnp.concatenate([(o / jnp.sum(p, -1, keepdims=True)).astype(BF16) for o, p in zip(outs, probs)], axis=1)
    z = DN_ALPHA * h32_ref[0] + _dot(attn, wo_ref[...])
    h = _ln(z, g_ref[...], b_ref[...], LN_EPS)
    o32_ref[0] = h
    o16_ref[0] = h.astype(BF16)


def _cross(h16, h32, kv, wq, wo, ln_g, ln_b):
    b, s, d = h32.shape
    tm = _tile(s, FUSED_ROW_TILE)
    row = pl.BlockSpec((1, tm, d), lambda bi, i: (bi, i, 0))
    wspec = pl.BlockSpec((d, d), lambda bi, i: (0, 0))
    vec = pl.BlockSpec((1, d), lambda bi, i: (0, 0))
    n_mem = kv.shape[1]
    return pl.pallas_call(
        _cross_kernel,
        grid=(b, s // tm),
        in_specs=[row, row, pl.BlockSpec((1, n_mem, 2 * d), lambda bi, i: (bi, 0, 0)), wspec, wspec, vec, vec],
        out_specs=[row, row],
        out_shape=[jax.ShapeDtypeStruct((b, s, d), F32), jax.ShapeDtypeStruct((b, s, d), BF16)],
        compiler_params=_params("parallel", "parallel"),
        name="cross_attn",
    )(h16, h32, kv, wq, wo, ln_g.reshape(1, d), ln_b.reshape(1, d))


def _mlp_kernel(h16_ref, h32_ref, w1_ref, w2_ref, g_ref, b_ref, o32_ref, o16_ref):
    x = h16_ref[...]
    dff = w1_ref.shape[1]
    tf = min(MLP_FF_TILE, dff)
    acc = DN_ALPHA * h32_ref[...]
    for f in range(dff // tf):
        a = jnp.maximum(_dot(x, w1_ref[:, f * tf:(f + 1) * tf]), 0.0)
        acc = acc + _dot((a * a).astype(BF16), w2_ref[f * tf:(f + 1) * tf, :])
    h = _ln(acc, g_ref[...], b_ref[...], LN_EPS)
    o32_ref[...] = h
    o16_ref[...] = h.astype(BF16)


def _mlp(h16, h32, w1, w2, ln_g, ln_b):
    t, d = h32.shape
    dff = w1.shape[1]
    tm = _tile(t, MLP_ROW_TILE)
    row = pl.BlockSpec((tm, d), lambda i: (i, 0))
    vec = pl.BlockSpec((1, d), lambda i: (0, 0))
    resident = pl.Buffered(1)
    return pl.pallas_call(
        _mlp_kernel,
        grid=(t // tm,),
        in_specs=[row, row, pl.BlockSpec((d, dff), lambda i: (0, 0), pipeline_mode=resident),
                  pl.BlockSpec((dff, d), lambda i: (0, 0), pipeline_mode=resident), vec, vec],
        out_specs=[row, row],
        out_shape=[jax.ShapeDtypeStruct((t, d), F32), jax.ShapeDtypeStruct((t, d), BF16)],
        compiler_params=_params("parallel"),
        name="mlp",
    )(h16, h32, w1, w2, ln_g.reshape(1, d), ln_b.reshape(1, d))


def _pad_rows(w, rows):
    return jnp.pad(w, ((0, rows - w.shape[0]), (0, 0)))


def _lora_layout(w):
    pad = lambda t, n: jnp.pad(t, [(0, 0)] * (t.ndim - 1) + [(0, n - t.shape[-1])])
    xw = w[..., :DECAY_LORA]
    xa = w[..., DECAY_LORA:DECAY_LORA + AAA_LORA]
    xg = w[..., DECAY_LORA + AAA_LORA:]
    return jnp.concatenate([pad(xw, 128), pad(xa, 128), pad(xg, 256)], axis=-1)


def kernel(x, mem, positions, ln_in_g, ln_in_b, w_in, b_gate, lam_q1, lam_k1, lam_q2, lam_k2, attn_subln_g, w_br_attn, pool_w, pool_scale, w_br_pool, rwkv_mu, rwkv_w0, rwkv_w2, rwkv_a0, rwkv_a2, rwkv_g2, rwkv_k_k, rwkv_k_a, rwkv_r_k, rwkv_lnx_g, rwkv_lnx_b, w_br_rwkv, w_out, ln1_g, ln1_b, w_xq, w_xkv, w_xo, ln2_g, ln2_b, w_ff1, w_ff2, ln3_g, ln3_b):
    bsz, seq, d = x.shape
    t = bsz * seq
    n_mem = mem.shape[1]
    depth = w_in.shape[0]

    inv_freq = 1.0 / (ROPE_THETA ** (jnp.arange(0, A_HEAD_DIM, 2, dtype=F32) / A_HEAD_DIM))
    ang = positions.astype(F32).reshape(t, 1) * inv_freq
    cos, sin = jnp.cos(ang), jnp.sin(ang)
    cos_t = jnp.concatenate([cos, cos, cos, cos], axis=-1)
    sin_t = jnp.concatenate([-sin, sin, -sin, sin], axis=-1)

    c_qk = 2 * A_HEADS * 2 * A_HEAD_DIM
    c_v = c_qk + A_HEADS * A_VAL_DIM
    c_pool = c_v + d
    c_rkv = c_pool + 3 * d
    c_lora = c_rkv + DECAY_LORA + AAA_LORA + GATE_LORA

    h32, h16 = _ln_in(x.reshape(t, d), ln_in_g, ln_in_b)
    mem2d = mem.reshape(bsz * n_mem, d)

    for l in range(depth):
        w = w_in[l]
        w16 = jnp.concatenate([w[:, :c_rkv], _lora_layout(w[:, c_rkv:c_lora]), w[:, c_lora:]], axis=1).astype(BF16)
        widths = (c_qk, c_v - c_qk, c_pool - c_v, c_rkv - c_pool, LORA_PAD, w.shape[1] - c_lora)
        mu = rwkv_mu[l]
        shift_mu = jnp.concatenate([mu[:3 * d], _lora_layout(mu[3 * d:])]).reshape(1, 3 * d + LORA_PAD)
        qk, vt, u_pool, u_rkv, u_lora, gates = _in_proj(h16, w16, cos_t, sin_t, b_gate[l].reshape(1, 3 * d), shift_mu,
                                                        widths, seq, min(ATTN_K_TILE, seq))
        qk = qk.reshape(bsz, seq, c_qk)
        u_pool = u_pool.reshape(bsz, seq, d)
        u_rkv = u_rkv.reshape(bsz, seq, 3 * d)
        u_lora = u_lora.reshape(bsz, seq, LORA_PAD)

        lam_init = 0.8 - 0.6 * math.exp(-0.3 * l)
        lam_rows = jnp.stack([lam_q1[l], lam_k1[l], lam_q2[l], lam_k2[l]]).astype(F32)
        y_a = _diff_attn(qk, vt, lam_rows, attn_subln_g[l], lam_init)

        y_b = _pool(u_pool, pool_w[l].astype(BF16), pool_scale[l])

        vecs = jnp.stack([rwkv_w0[l], rwkv_a0[l], rwkv_k_k[l], rwkv_k_a[l], rwkv_r_k[l].reshape(d),
                          rwkv_lnx_g[l], rwkv_lnx_b[l], jnp.zeros((d,), F32)])
        w2p = _pad_rows(rwkv_w2[l], 128).astype(BF16)
        a2p = _pad_rows(rwkv_a2[l], 128).astype(BF16)
        g2p = _pad_rows(rwkv_g2[l], 256).astype(BF16)
        y_c = _rwkv(u_rkv, u_lora, vecs, w2p, a2p, g2p)

        h32, h16 = _merge(y_a.reshape(t, d), y_b.reshape(t, d), y_c.reshape(t, d), gates, h32,
                          w_br_attn[l].astype(BF16), w_br_pool[l].astype(BF16), w_br_rwkv[l].astype(BF16),
                          w_out[l].astype(BF16), ln1_g[l], ln1_b[l])

        kv = _proj(mem2d, w_xkv[l].astype(BF16), BF16, name="proj_kv").reshape(bsz, n_mem, 2 * d)
        h32, h16 = _cross(h16.reshape(bsz, seq, d), h32.reshape(bsz, seq, d), kv, w_xq[l].astype(BF16),
                          w_xo[l].astype(BF16), ln2_g[l], ln2_b[l])
        h32, h16 = h32.reshape(t, d), h16.reshape(t, d)

        h32, h16 = _mlp(h16, h32, w_ff1[l].astype(BF16), w_ff2[l].astype(BF16), ln3_g[l], ln3_b[l])

    return h32.reshape(bsz, seq, d)
```

```python
import functools
import math

import jax
import jax.numpy as jnp
from jax import lax
from jax.experimental import pallas as pl
from jax.experimental.pallas import tpu as pltpu

F32 = jnp.float32
BF16 = jnp.bfloat16

D_MODEL = 1024
DEPTH = 2
CHUNK = 64
A_HEADS = 8
A_HEAD_DIM = 64
A_VAL_DIM = 2 * A_HEAD_DIM
ROPE_THETA = 10000.0
POOL_WINDOWS = (2, 4, 8, 16)
POOL_GROUP_DIM = D_MODEL // len(POOL_WINDOWS)
R_HEAD_DIM = 64
DECAY_LORA = 64
AAA_LORA = 64
GATE_LORA = 160
X_HEADS = 4
X_HEAD_DIM = D_MODEL // X_HEADS
DN_ALPHA = (2 * DEPTH) ** 0.25
LN_EPS = 1e-5
RMS_EPS = 1e-5
RWKV_GN_EPS = 64e-5
KK_NORM_EPS = 1e-12
NEG_INF = -1e30
LOG2_E = math.log2(math.e)
ONES_ROWS = 16

V7X_LANES = 128
V7X_MXU_DIM = 256
V7X_VMEM_LIMIT_BYTES = 56 * 1024 * 1024

ROW_TILE = 1024
COL_TILE = 1024
IN_PROJ_ROW_TILE = 512
IN_PROJ_CHUNK = 1024
FUSED_ROW_TILE = 512
MLP_ROW_TILE = 512
MLP_FF_TILE = 1024
ATTN_Q_TILE = 256
ATTN_K_TILE = 256
ATTN_HEADS_PER_STEP = 2
RW_CHUNK = 64
RW_QUAD = V7X_MXU_DIM
RW_TILE = 256
HALO_ROWS = 16
LORA_PAD = 512


def _tile(n, pref):
    return pref if n % pref == 0 else n


def _params(*sem):
    return pltpu.CompilerParams(dimension_semantics=sem, vmem_limit_bytes=V7X_VMEM_LIMIT_BYTES)


def _ln(z, g, b, eps):
    mu = jnp.mean(z, -1, keepdims=True)
    d = z - mu
    var = jnp.mean(d * d, -1, keepdims=True)
    return d * lax.rsqrt(var + eps) * g + b


_DIMS = {"nn": (((1,), (0,)), ((), ())), "nt": (((1,), (1,)), ((), ())), "tn": (((0,), (0,)), ((), ()))}


def _dot(a, b):
    return jnp.dot(a, b, preferred_element_type=F32)


def _dot_nt(a, b):
    return lax.dot_general(a, b, _DIMS["nt"], preferred_element_type=F32)


def _sigmoid(x):
    return 0.5 * jnp.tanh(0.5 * x) + 0.5


def _split(x):
    hi = x.astype(BF16)
    return hi, (x - hi.astype(F32)).astype(BF16)


def _mm(a, b, kind="nn"):
    return lax.dot_general(a.astype(BF16), b.astype(BF16), _DIMS[kind], preferred_element_type=F32)


def _ln_in_kernel(x_ref, g_ref, b_ref, h32_ref, h16_ref):
    h = _ln(x_ref[...], g_ref[...], b_ref[...], LN_EPS)
    h32_ref[...] = h
    h16_ref[...] = h.astype(BF16)


def _ln_in(x2d, g, b):
    t, d = x2d.shape
    tm = _tile(t, ROW_TILE)
    row = pl.BlockSpec((tm, d), lambda i: (i, 0))
    vec = pl.BlockSpec((1, d), lambda i: (0, 0))
    return pl.pallas_call(
        _ln_in_kernel,
        grid=(t // tm,),
        in_specs=[row, vec, vec],
        out_specs=[row, row],
        out_shape=[jax.ShapeDtypeStruct((t, d), F32), jax.ShapeDtypeStruct((t, d), BF16)],
        compiler_params=_params("parallel"),
        name="ln_in",
    )(x2d, g.reshape(1, d), b.reshape(1, d))


def _proj_kernel(x_ref, w_ref, o_ref):
    o_ref[...] = _dot(x_ref[...].astype(BF16), w_ref[...]).astype(o_ref.dtype)


def _proj(x, w16, out_dtype, name):
    m, k = x.shape
    n = w16.shape[1]
    tm = _tile(m, ROW_TILE)
    tn = _tile(n, COL_TILE)
    return pl.pallas_call(
        _proj_kernel,
        grid=(m // tm, n // tn),
        in_specs=[pl.BlockSpec((tm, k), lambda i, j: (i, 0)), pl.BlockSpec((k, tn), lambda i, j: (0, j))],
        out_specs=pl.BlockSpec((tm, tn), lambda i, j: (i, j)),
        out_shape=jax.ShapeDtypeStruct((m, n), out_dtype),
        compiler_params=_params("parallel", "parallel"),
        name=name,
    )(x, w16)


def _rope(t, cos, sin, first_half):
    partner = jnp.where(first_half, pltpu.roll(t, V7X_LANES - A_HEAD_DIM // 2, 1), pltpu.roll(t, A_HEAD_DIM // 2, 1))
    return t * cos + partner * sin


def _in_proj_kernel(x_ref, halo_ref, w_ref, cos_ref, sin_ref, bias_ref, mu_ref,
                    qk_ref, v_ref, pool_ref, rkv_ref, lora_ref, gate_ref, *, tiles_per_seq):
    x = x_ref[...]
    at_start = pl.program_id(0) % tiles_per_seq == 0
    halo = halo_ref[...]
    x_ext = jnp.concatenate([jnp.where(at_start, jnp.zeros_like(halo), halo), x], axis=0)
    cos = cos_ref[...]
    sin = sin_ref[...]
    lane = lax.broadcasted_iota(jnp.int32, cos.shape, 1)
    first_half = (lane & (A_HEAD_DIM - 1)) < A_HEAD_DIM // 2
    col = 0
    mu_col = 0
    for o_ref in (qk_ref, v_ref, pool_ref, rkv_ref, lora_ref, gate_ref):
        width = A_HEADS * A_VAL_DIM if o_ref is v_ref else o_ref.shape[1]
        for c0 in range(0, width, IN_PROJ_CHUNK):
            c1 = min(c0 + IN_PROJ_CHUNK, width)
            if o_ref is rkv_ref or o_ref is lora_ref:
                t_ext = _dot(x_ext, w_ref[:, col + c0:col + c1])
                t = t_ext[HALO_ROWS:]
                prev = pltpu.roll(t_ext, 1, 0)[HALO_ROWS:]
                mu = mu_ref[:, mu_col + c0:mu_col + c1]
                o_ref[:, c0:c1] = (t + (prev - t) * mu).astype(o_ref.dtype)
                continue
            t = _dot(x, w_ref[:, col + c0:col + c1])
            if o_ref is qk_ref:
                for j in range(c0, c1, V7X_LANES):
                    blk = t[:, j - c0:j - c0 + V7X_LANES]
                    o_ref[:, j:j + V7X_LANES] = _rope(blk, cos, sin, first_half).astype(o_ref.dtype)
            elif o_ref is gate_ref:
                o_ref[:, c0:c1] = jax.nn.sigmoid(t + bias_ref[:, c0:c1]).astype(o_ref.dtype)
            elif o_ref is v_ref:
                tk = v_ref.shape[4]
                for h in range((c1 - c0) // A_VAL_DIM):
                    for kt in range(v_ref.shape[2]):
                        blk = t[kt * tk:(kt + 1) * tk, h * A_VAL_DIM:(h + 1) * A_VAL_DIM]
                        head = c0 // A_VAL_DIM + h
                        v_ref[0, head, kt, :A_VAL_DIM, :] = blk.T.astype(v_ref.dtype)
                        v_ref[0, head, kt, A_VAL_DIM:, :] = jnp.ones((ONES_ROWS, tk), v_ref.dtype)
            else:
                o_ref[:, c0:c1] = t.astype(o_ref.dtype)
        col += width
        if o_ref is rkv_ref or o_ref is lora_ref:
            mu_col += width


def _in_proj(x16, w16, cos_t, sin_t, gate_bias, shift_mu, widths, seq, tk):
    m, k = x16.shape
    tm = _tile(seq, IN_PROJ_ROW_TILE)
    tiles_per_seq = seq // tm
    row = lambda n: pl.BlockSpec((tm, n), lambda i: (i, 0))
    vt_rows = A_VAL_DIM + ONES_ROWS
    vt_spec = pl.BlockSpec((1, A_HEADS, tm // tk, vt_rows, tk),
                           lambda i: (i // tiles_per_seq, 0, i % tiles_per_seq, 0, 0))
    out_specs = [row(n) for n in widths]
    out_shape = [jax.ShapeDtypeStruct((m, n), BF16) for n in widths]
    out_specs[1] = vt_spec
    out_shape[1] = jax.ShapeDtypeStruct((m // seq, A_HEADS, seq // tk, vt_rows, tk), BF16)
    halo_blocks = tm // HALO_ROWS
    halo = pl.BlockSpec((HALO_ROWS, k), lambda i: (jnp.maximum(i * halo_blocks - 1, 0), 0))
    const = lambda a: pl.BlockSpec(a.shape, lambda i: (0, 0))
    return pl.pallas_call(
        functools.partial(_in_proj_kernel, tiles_per_seq=tiles_per_seq),
        grid=(m // tm,),
        in_specs=[row(k), halo, pl.BlockSpec(w16.shape, lambda i: (0, 0), pipeline_mode=pl.Buffered(1)),
                  row(V7X_LANES), row(V7X_LANES), const(gate_bias), const(shift_mu)],
        out_specs=out_specs,
        out_shape=out_shape,
        compiler_params=_params("parallel"),
        name="in_proj",
    )(x16, x16, w16, cos_t, sin_t, gate_bias, shift_mu)


def _diff_attn_kernel(lam_ref, g_ref, q_ref, k_ref, v_ref, o_ref, s_buf, p_buf, *, lam_init):
    i = pl.program_id(2)
    tq = q_ref.shape[1]
    tk = s_buf.shape[2]
    n_heads = s_buf.shape[0]
    head_cols = [slice(h * A_VAL_DIM, (h + 1) * A_VAL_DIM) for h in range(n_heads)]

    def stacked_queries(cols):
        q = (q_ref[0, :, cols].astype(F32) * (A_HEAD_DIM ** -0.5 * LOG2_E)).astype(BF16)
        lane = lax.broadcasted_iota(jnp.int32, q.shape, 1)
        zero = jnp.zeros_like(q)
        return jnp.concatenate([jnp.where(lane < A_HEAD_DIM, q, zero), jnp.where(lane >= A_HEAD_DIM, q, zero)], axis=0)

    qs = [stacked_queries(cols) for cols in head_cols]

    def rows(j):
        return pl.ds(pl.multiple_of(j * tk, tk), tk)

    def next_scores(h, j):
        return _dot_nt(k_ref[0, rows(j), head_cols[h]], qs[h])

    def values_product(h, j, p):
        return _dot(v_ref[0, h, j], p)

    def softmax_step(j, m, s, masked):
        if masked:
            q_off = lax.broadcasted_iota(jnp.int32, (1, 2 * tq), 1) & (tq - 1)
            delta = (i * tq - j * tk) // CHUNK
            s = jnp.concatenate([jnp.where(q_off >= (a - delta) * CHUNK, s[a * CHUNK:(a + 1) * CHUNK], NEG_INF)
                                 for a in range(tk // CHUNK)], axis=0)
        m_new = jnp.maximum(m, jnp.max(s, axis=0, keepdims=True))
        return m_new, jnp.exp2(m - m_new), jnp.exp2(s - m_new).astype(BF16)

    def body(j, carries):
        slot = j & 1
        loaded = [(s_buf[h, slot], p_buf[h, 1 - slot]) for h in range(n_heads)]
        pvs = [values_product(h, jnp.maximum(j - 1, 0), loaded[h][1]) for h in range(n_heads)]
        s_next = [next_scores(h, j + 1) for h in range(n_heads)]
        stepped = [softmax_step(j, carries[h][0], loaded[h][0], False) for h in range(n_heads)]
        for h, (_, _, p) in enumerate(stepped):
            s_buf[h, 1 - slot] = s_next[h]
            p_buf[h, slot] = p
        return tuple((m_new, alpha * (carries[h][1] + pvs[h])) for h, (m_new, alpha, _) in enumerate(stepped))

    for h in range(n_heads):
        s_buf[h, 0] = next_scores(h, 0)
        p_buf[h, 1] = jnp.zeros(p_buf.shape[2:], BF16)
    init = (jnp.full((1, 2 * tq), NEG_INF, F32), jnp.zeros((v_ref.shape[3], 2 * tq), F32))
    n_full = (i * tq) // tk
    carries = lax.fori_loop(0, n_full, body, (init,) * n_heads)

    lam_rows = lam_ref[...]
    lam = (jnp.exp(jnp.sum(lam_rows[0:1] * lam_rows[1:2], -1, keepdims=True))
           - jnp.exp(jnp.sum(lam_rows[2:3] * lam_rows[3:4], -1, keepdims=True)) + lam_init)
    last = n_full & 1
    for h in range(n_heads):
        pv = values_product(h, jnp.maximum(n_full - 1, 0), p_buf[h, 1 - last])
        _, alpha, p = softmax_step(n_full, carries[h][0], s_buf[h, last], True)
        acc = alpha * (carries[h][1] + pv) + values_product(h, n_full, p)
        out = acc[:A_VAL_DIM] / acc[A_VAL_DIM:A_VAL_DIM + 1]
        o = out[:, :tq] - lam * out[:, tq:]
        o = o * lax.rsqrt(jnp.mean(o * o, axis=0, keepdims=True) + RMS_EPS) * g_ref[...]
        o_ref[0, :, head_cols[h]] = (o * (1.0 - lam_init)).T.astype(o_ref.dtype)


def _diff_attn(qk, vt, lam_rows, subln_g, lam_init):
    b, _, n_ktiles, vt_rows, tk = vt.shape
    s = n_ktiles * tk
    tq = _tile(s, ATTN_Q_TILE)
    nh = ATTN_HEADS_PER_STEP
    width = nh * A_VAL_DIM
    k_blk0 = A_HEADS // nh
    return pl.pallas_call(
        functools.partial(_diff_attn_kernel, lam_init=lam_init),
        grid=(b, A_HEADS // nh, s // tq),
        in_specs=[
            pl.BlockSpec((4, A_HEAD_DIM), lambda bi, h, i: (0, 0)),
            pl.BlockSpec((A_VAL_DIM, 1), lambda bi, h, i: (0, 0)),
            pl.BlockSpec((1, tq, width), lambda bi, h, i: (bi, i, h)),
            pl.BlockSpec((1, s, width), lambda bi, h, i: (bi, 0, k_blk0 + h)),
            pl.BlockSpec((1, nh, n_ktiles, vt_rows, tk), lambda bi, h, i: (bi, h, 0, 0, 0)),
        ],
        out_specs=pl.BlockSpec((1, tq, width), lambda bi, h, i: (bi, i, h)),
        out_shape=jax.ShapeDtypeStruct((b, s, A_HEADS * A_VAL_DIM), BF16),
        scratch_shapes=[pltpu.VMEM((nh, 2, tk, 2 * tq), F32), pltpu.VMEM((nh, 2, tk, 2 * tq), BF16)],
        compiler_params=_params("parallel", "parallel", "arbitrary"),
        name="diff_attn",
    )(lam_rows, subln_g.reshape(A_VAL_DIM, 1), qk, qk, vt)


def _pool_kernel(p_ref, w_ref, scale_ref, o_ref):
    g = pl.program_id(1)
    x = p_ref[0].astype(F32)
    row = lax.broadcasted_iota(jnp.int32, x.shape, 0)
    for gi, window in enumerate(POOL_WINDOWS):
        @pl.when(g == gi)
        def _(window=window):
            ws = x
            shift = 1
            while shift < window:
                ws = ws + jnp.where(row >= shift, pltpu.roll(ws, shift, 0), 0.0)
                shift *= 2
            count = jnp.minimum(row + 1, window).astype(F32)
            d = ws / count - x
            y = _dot(d.astype(BF16), w_ref[0]) * scale_ref[...]
            o_ref[0] = y.astype(o_ref.dtype)


def _pool(u_pool, pool_w16, pool_scale):
    b, s, d = u_pool.shape
    cg = POOL_GROUP_DIM
    return pl.pallas_call(
        _pool_kernel,
        grid=(b, len(POOL_WINDOWS)),
        in_specs=[
            pl.BlockSpec((1, s, cg), lambda bi, g: (bi, 0, g)),
            pl.BlockSpec((1, cg, cg), lambda bi, g: (g, 0, 0)),
            pl.BlockSpec((1, cg), lambda bi, g: (0, g)),
        ],
        out_specs=pl.BlockSpec((1, s, cg), lambda bi, g: (bi, 0, g)),
        out_shape=jax.ShapeDtypeStruct((b, s, d), BF16),
        compiler_params=_params("parallel", "parallel"),
        name="pool_mixer",
    )(u_pool, pool_w16, pool_scale.reshape(1, d))


def _block_mask():
    r = lax.broadcasted_iota(jnp.int32, (RW_QUAD, RW_QUAD), 0)
    c = lax.broadcasted_iota(jnp.int32, (RW_QUAD, RW_QUAD), 1)
    return (r // R_HEAD_DIM) == (c // R_HEAD_DIM)


def _bd(x, mask):
    tiled = jnp.concatenate([x] * (RW_QUAD // R_HEAD_DIM), axis=0)
    return jnp.where(mask, tiled, jnp.zeros_like(tiled))


def _seg_quad(x, ones_bd):
    hi, lo = _split(x)
    return _dot(hi, ones_bd) + _dot(lo, ones_bd)


def _seg_sum(x, ones_bd):
    parts = [_seg_quad(x[:, q * RW_QUAD:(q + 1) * RW_QUAD], ones_bd) for q in range(x.shape[1] // RW_QUAD)]
    return jnp.concatenate(parts, axis=1)


def _chunk_summaries(units, mask):
    c = RW_CHUNK
    ii = lax.broadcasted_iota(jnp.int32, (c, RW_QUAD), 0)
    jj = lax.broadcasted_iota(jnp.int32, (c, RW_QUAD), 1) % c
    strict = ii > jj
    incl = ii >= jj
    eye = (ii == jj).astype(F32)
    bd = lambda x: _bd(x, mask)

    scaled = []
    for r, k, v, a, b, lw, cum in units:
        cum_end = cum[c - 1:c, :]
        at = a * jnp.exp(cum - lw)
        rt = r * jnp.exp(cum)
        e_neg = jnp.exp(-cum)
        w_end = jnp.exp(cum_end)
        bt = b * e_neg
        kt = k * e_neg
        scaled.append((at, rt, bt, kt, bt * w_end, kt * w_end, v, w_end))

    scores = [_mm(jnp.concatenate([at, rt], axis=0), jnp.concatenate([bd(bt), bd(kt)], axis=0), "nt")
              for at, rt, bt, kt, _, _, _, _ in scaled]
    a_ab = [jnp.where(strict, sc[:c, :RW_QUAD], 0.0) for sc in scores]
    a_rb = [jnp.where(incl, sc[c:, :RW_QUAD], 0.0) for sc in scores]
    av = [_mm(jnp.concatenate([jnp.where(strict, sc[:c, RW_QUAD:], 0.0), jnp.where(incl, sc[c:, RW_QUAD:], 0.0)],
                              axis=0), bd(u[6])) for sc, u in zip(scores, scaled)]

    tinv = [a + eye for a in a_ab]
    pk = [_mm(a, bd(a)) for a in a_ab]
    for _ in range(int(math.log2(c)) - 2):
        both = [_mm(jnp.concatenate([p, t], axis=0), bd(p)) for p, t in zip(pk, tinv)]
        tinv = [t + b[c:] for t, b in zip(tinv, both)]
        pk = [b[:c] for b in both]
    tinv = [t + _mm(t, bd(p)) for t, p in zip(tinv, pk)]

    side = lambda x, y: jnp.concatenate([x, y], axis=1)
    out = []
    for t, arb, avu, (at, rt, _, _, bh, kh, v, w_end) in zip(tinv, a_rb, av, scaled):
        x = _mm(t, side(bd(at), bd(avu[:c])))
        xa, xu = x[:, :RW_QUAD], x[:, RW_QUAD:]
        ry = _mm(arb, side(bd(xa), bd(xu)))
        z = _mm(bh, x, "tn")
        z1 = jnp.where(mask, z[:, :RW_QUAD], 0.0)
        z2 = jnp.where(mask, z[:, RW_QUAD:] + _mm(kh, v, "tn"), 0.0)
        fold = lambda z: functools.reduce(jnp.add, [z[h * c:(h + 1) * c] for h in range(RW_QUAD // R_HEAD_DIM)])
        out.append((rt + ry[:, :RW_QUAD], avu[c:] + ry[:, RW_QUAD:], fold(z1) + eye * w_end, fold(z2)))
    return out


def _rwkv_kernel(rkv_ref, lora_ref, vec_ref, w2_ref, a2_ref, g2_ref, o_ref, state_ref):
    ts = rkv_ref.shape[1]
    d = D_MODEL

    @pl.when(pl.program_id(1) == 0)
    def _():
        state_ref[...] = jnp.zeros_like(state_ref)

    xs = rkv_ref[0].astype(F32)
    ls = lora_ref[0].astype(F32)
    r = xs[:, :d]
    k = xs[:, d:2 * d]
    v = xs[:, 2 * d:]
    vec = vec_ref[...]
    w0, a0, k_k, k_a, r_k = vec[0:1], vec[1:2], vec[2:3], vec[3:4], vec[4:5]

    z = w0 + _dot(jnp.tanh(ls[:, :128]).astype(BF16), w2_ref[...])
    lw = -math.exp(-0.5) * _sigmoid(z)
    a_sig = _sigmoid(a0 + _dot(ls[:, 128:256].astype(BF16), a2_ref[...]))
    gate = _dot(_sigmoid(ls[:, 256:]).astype(BF16), g2_ref[...])

    mask = _block_mask()
    ones_bd = mask.astype(BF16)
    kk = k * k_k
    kk = kk * lax.rsqrt(jnp.maximum(_seg_sum(kk * kk, ones_bd), KK_NORM_EPS * KK_NORM_EPS))
    k2 = k * (1.0 + (a_sig - 1.0) * k_a)
    a_vec = -kk
    b_vec = kk * a_sig
    bonus = _seg_sum(r * k2 * r_k, ones_bd) * v

    ti = lax.broadcasted_iota(jnp.int32, (ts, ts), 0)
    tj = lax.broadcasted_iota(jnp.int32, (ts, ts), 1)
    tri = ((tj <= ti) & (tj // RW_CHUNK == ti // RW_CHUNK)).astype(BF16)
    lw_hi, lw_lo = _split(lw)
    cum = _dot(tri, lw_hi) + _dot(tri, lw_lo)

    windows = [(slice(c * RW_CHUNK, (c + 1) * RW_CHUNK), slice(q * RW_QUAD, (q + 1) * RW_QUAD))
               for c in range(ts // RW_CHUNK) for q in range(d // RW_QUAD)]
    units = [tuple(x[rows, cols] for x in (r, k2, v, a_vec, b_vec, lw, cum)) for rows, cols in windows]
    summaries = _chunk_summaries(units, mask)

    inv_n = 1.0 / R_HEAD_DIM
    lnx_g, lnx_b = vec[5:6], vec[6:7]
    n_quads = d // RW_QUAD
    states = [state_ref[:, q * RW_QUAD:(q + 1) * RW_QUAD] for q in range(n_quads)]
    start_states = []
    for (rows, cols), (rp, y0, m, n) in zip(windows, summaries):
        q = cols.start // RW_QUAD
        start_bd = _bd(states[q], mask)
        start_states.append(start_bd)
        states[q] = _mm(m, start_bd) + n
    for q in range(n_quads):
        state_ref[:, q * RW_QUAD:(q + 1) * RW_QUAD] = states[q]

    ys = [_mm(rp, h0) + y0 for (rp, y0, _, _), h0 in zip(summaries, start_states)]
    mus = [_seg_quad(y, ones_bd) * inv_n for y in ys]
    dys = [y - mu for y, mu in zip(ys, mus)]
    variances = [_seg_quad(dy * dy, ones_bd) * inv_n for dy in dys]
    for (rows, cols), dy, var in zip(windows, dys, variances):
        yn = dy * lax.rsqrt(var + RWKV_GN_EPS) * lnx_g[:, cols] + lnx_b[:, cols]
        o_ref[0, rows, cols] = ((yn + bonus[rows, cols]) * gate[rows, cols]).astype(o_ref.dtype)


def _rwkv(u_rkv, u_lora, vecs, w2p, a2p, g2p):
    b, s, w3 = u_rkv.shape
    d = D_MODEL
    ts = _tile(s, RW_TILE)

    def tile_spec(width):
        return pl.BlockSpec((1, ts, width), lambda bi, t: (bi, t, 0))

    def const_spec(shape):
        return pl.BlockSpec(shape, lambda bi, t: (0,) * len(shape))

    return pl.pallas_call(
        _rwkv_kernel,
        grid=(b, s // ts),
        in_specs=[tile_spec(w3), tile_spec(LORA_PAD), const_spec(vecs.shape),
                  const_spec(w2p.shape), const_spec(a2p.shape), const_spec(g2p.shape)],
        out_specs=tile_spec(d),
        out_shape=jax.ShapeDtypeStruct((b, s, d), BF16),
        scratch_shapes=[pltpu.VMEM((R_HEAD_DIM, d), F32)],
        compiler_params=_params("parallel", "arbitrary"),
        name="rwkv7",
    )(u_rkv, u_lora, vecs, w2p, a2p, g2p)


def _merge_kernel(ya_ref, yb_ref, yc_ref, gates_ref, h_ref, wa_ref, wb_ref, wc_ref, wo_ref,
                  g_ref, b_ref, h32_ref, h16_ref):
    d = D_MODEL
    merged = jnp.zeros((ya_ref.shape[0], d), F32)
    for idx, (y_ref, w_ref) in enumerate(((ya_ref, wa_ref), (yb_ref, wb_ref), (yc_ref, wc_ref))):
        gate = gates_ref[:, idx * d:(idx + 1) * d].astype(F32)
        merged = merged + gate * _dot(y_ref[...], w_ref[...])
    z = DN_ALPHA * h_ref[...] + _dot(merged.astype(BF16), wo_ref[...])
    h = _ln(z, g_ref[...], b_ref[...], LN_EPS)
    h32_ref[...] = h
    h16_ref[...] = h.astype(BF16)


def _merge(ya, yb, yc, gates, h32, wa, wb, wc, wo, ln_g, ln_b):
    t, d = h32.shape
    tm = _tile(t, FUSED_ROW_TILE)
    row = pl.BlockSpec((tm, d), lambda i: (i, 0))
    wspec = pl.BlockSpec((d, d), lambda i: (0, 0))
    vec = pl.BlockSpec((1, d), lambda i: (0, 0))
    return pl.pallas_call(
        _merge_kernel,
        grid=(t // tm,),
        in_specs=[row, row, row, pl.BlockSpec((tm, 3 * d), lambda i: (i, 0)),
                  row, wspec, wspec, wspec, wspec, vec, vec],
        out_specs=[row, row],
        out_shape=[jax.ShapeDtypeStruct((t, d), F32), jax.ShapeDtypeStruct((t, d), BF16)],
        compiler_params=_params("parallel"),
        name="merge_out",
    )(ya, yb, yc, gates, h32, wa, wb, wc, wo, ln_g.reshape(1, d), ln_b.reshape(1, d))


def _cross_kernel(h16_ref, h32_ref, kv_ref, wq_ref, wo_ref, g_ref, b_ref, o32_ref, o16_ref):
    d = D_MODEL
    q = _dot(h16_ref[0], wq_ref[...]).astype(BF16)
    kv = kv_ref[0]
    head_cols = [slice(hd * X_HEAD_DIM, (hd + 1) * X_HEAD_DIM) for hd in range(X_HEADS)]
    scores = [_dot_nt(q[:, cols], kv[:, cols]) * (X_HEAD_DIM ** -0.5) for cols in head_cols]
    probs = [jnp.exp(s - jnp.max(s, -1, keepdims=True)) for s in scores]
    outs = [_dot(p.astype(BF16), kv[:, d + cols.start:d + cols.stop]) for p, cols in zip(probs, head_cols)]
    attn = jnp.concatenate([(o / jnp.sum(p, -1, keepdims=True)).astype(BF16) for o, p in zip(outs, probs)], axis=1)
    z = DN_ALPHA * h32_ref[0] + _dot(attn, wo_ref[...])
    h = _ln(z, g_ref[...], b_ref[...], LN_EPS)
    o32_ref[0] = h
    o16_ref[0] = h.astype(BF16)


def _cross(h16, h32, kv, wq, wo, ln_g, ln_b):
    b, s, d = h32.shape
    tm = _tile(s, FUSED_ROW_TILE)
    row = pl.BlockSpec((1, tm, d), lambda bi, i: (bi, i, 0))
    wspec = pl.BlockSpec((d, d), lambda bi, i: (0, 0))
    vec = pl.BlockSpec((1, d), lambda bi, i: (0, 0))
    n_mem = kv.shape[1]
    return pl.pallas_call(
        _cross_kernel,
        grid=(b, s // tm),
        in_specs=[row, row, pl.BlockSpec((1, n_mem, 2 * d), lambda bi, i: (bi, 0, 0)), wspec, wspec, vec, vec],
        out_specs=[row, row],
        out_shape=[jax.ShapeDtypeStruct((b, s, d), F32), jax.ShapeDtypeStruct((b, s, d), BF16)],
        compiler_params=_params("parallel", "parallel"),
        name="cross_attn",
    )(h16, h32, kv, wq, wo, ln_g.reshape(1, d), ln_b.reshape(1, d))


def _mlp_kernel(h16_ref, h32_ref, w1_ref, w2_ref, g_ref, b_ref, o32_ref, o16_ref):
    x = h16_ref[...]
    dff = w1_ref.shape[1]
    tf = min(MLP_FF_TILE, dff)
    acc = DN_ALPHA * h32_ref[...]
    for f in range(dff // tf):
        a = jnp.maximum(_dot(x, w1_ref[:, f * tf:(f + 1) * tf]), 0.0)
        acc = acc + _dot((a * a).astype(BF16), w2_ref[f * tf:(f + 1) * tf, :])
    h = _ln(acc, g_ref[...], b_ref[...], LN_EPS)
    o32_ref[...] = h
    o16_ref[...] = h.astype(BF16)


def _mlp(h16, h32, w1, w2, ln_g, ln_b):
    t, d = h32.shape
    dff = w1.shape[1]
    tm = _tile(t, MLP_ROW_TILE)
    row = pl.BlockSpec((tm, d), lambda i: (i, 0))
    vec = pl.BlockSpec((1, d), lambda i: (0, 0))
    resident = pl.Buffered(1)
    return pl.pallas_call(
        _mlp_kernel,
        grid=(t // tm,),
        in_specs=[row, row, pl.BlockSpec((d, dff), lambda i: (0, 0), pipeline_mode=resident),
                  pl.BlockSpec((dff, d), lambda i: (0, 0), pipeline_mode=resident), vec, vec],
        out_specs=[row, row],
        out_shape=[jax.ShapeDtypeStruct((t, d), F32), jax.ShapeDtypeStruct((t, d), BF16)],
        compiler_params=_params("parallel"),
        name="mlp",
    )(h16, h32, w1, w2, ln_g.reshape(1, d), ln_b.reshape(1, d))


def _pad_rows(w, rows):
    return jnp.pad(w, ((0, rows - w.shape[0]), (0, 0)))


def _lora_layout(w):
    pad = lambda t, n: jnp.pad(t, [(0, 0)] * (t.ndim - 1) + [(0, n - t.shape[-1])])
    xw = w[..., :DECAY_LORA]
    xa = w[..., DECAY_LORA:DECAY_LORA + AAA_LORA]
    xg = w[..., DECAY_LORA + AAA_LORA:]
    return jnp.concatenate([pad(xw, 128), pad(xa, 128), pad(xg, 256)], axis=-1)


def kernel(x, mem, positions, ln_in_g, ln_in_b, w_in, b_gate, lam_q1, lam_k1, lam_q2, lam_k2, attn_subln_g, w_br_attn, pool_w, pool_scale, w_br_pool, rwkv_mu, rwkv_w0, rwkv_w2, rwkv_a0, rwkv_a2, rwkv_g2, rwkv_k_k, rwkv_k_a, rwkv_r_k, rwkv_lnx_g, rwkv_lnx_b, w_br_rwkv, w_out, ln1_g, ln1_b, w_xq, w_xkv, w_xo, ln2_g, ln2_b, w_ff1, w_ff2, ln3_g, ln3_b):
    bsz, seq, d = x.shape
    t = bsz * seq
    n_mem = mem.shape[1]
    depth = w_in.shape[0]

    inv_freq = 1.0 / (ROPE_THETA ** (jnp.arange(0, A_HEAD_DIM, 2, dtype=F32) / A_HEAD_DIM))
    ang = positions.astype(F32).reshape(t, 1) * inv_freq
    cos, sin = jnp.cos(ang), jnp.sin(ang)
    cos_t = jnp.concatenate([cos, cos, cos, cos], axis=-1)
    sin_t = jnp.concatenate([-sin, sin, -sin, sin], axis=-1)

    c_qk = 2 * A_HEADS * 2 * A_HEAD_DIM
    c_v = c_qk + A_HEADS * A_VAL_DIM
    c_pool = c_v + d
    c_rkv = c_pool + 3 * d
    c_lora = c_rkv + DECAY_LORA + AAA_LORA + GATE_LORA

    h32, h16 = _ln_in(x.reshape(t, d), ln_in_g, ln_in_b)
    mem2d = mem.reshape(bsz * n_mem, d)

    for l in range(depth):
        w = w_in[l]
        w16 = jnp.concatenate([w[:, :c_rkv], _lora_layout(w[:, c_rkv:c_lora]), w[:, c_lora:]], axis=1).astype(BF16)
        widths = (c_qk, c_v - c_qk, c_pool - c_v, c_rkv - c_pool, LORA_PAD, w.shape[1] - c_lora)
        mu = rwkv_mu[l]
        shift_mu = jnp.concatenate([mu[:3 * d], _lora_layout(mu[3 * d:])]).reshape(1, 3 * d + LORA_PAD)
        qk, vt, u_pool, u_rkv, u_lora, gates = _in_proj(h16, w16, cos_t, sin_t, b_gate[l].reshape(1, 3 * d), shift_mu,
                                                        widths, seq, min(ATTN_K_TILE, seq))
        qk = qk.reshape(bsz, seq, c_qk)
        u_pool = u_pool.reshape(bsz, seq, d)
        u_rkv = u_rkv.reshape(bsz, seq, 3 * d)
        u_lora = u_lora.reshape(bsz, seq, LORA_PAD)

        lam_init = 0.8 - 0.6 * math.exp(-0.3 * l)
        lam_rows = jnp.stack([lam_q1[l], lam_k1[l], lam_q2[l], lam_k2[l]]).astype(F32)
        y_a = _diff_attn(qk, vt, lam_rows, attn_subln_g[l], lam_init)

        y_b = _pool(u_pool, pool_w[l].astype(BF16), pool_scale[l])

        vecs = jnp.stack([rwkv_w0[l], rwkv_a0[l], rwkv_k_k[l], rwkv_k_a[l], rwkv_r_k[l].reshape(d),
                          rwkv_lnx_g[l], rwkv_lnx_b[l], jnp.zeros((d,), F32)])
        w2p = _pad_rows(rwkv_w2[l], 128).astype(BF16)
        a2p = _pad_rows(rwkv_a2[l], 128).astype(BF16)
        g2p = _pad_rows(rwkv_g2[l], 256).astype(BF16)
        y_c = _rwkv(u_rkv, u_lora, vecs, w2p, a2p, g2p)

        h32, h16 = _merge(y_a.reshape(t, d), y_b.reshape(t, d), y_c.reshape(t, d), gates, h32,
                          w_br_attn[l].astype(BF16), w_br_pool[l].astype(BF16), w_br_rwkv[l].astype(BF16),
                          w_out[l].astype(BF16), ln1_g[l], ln1_b[l])

        kv = _proj(mem2d, w_xkv[l].astype(BF16), BF16, name="proj_kv").reshape(bsz, n_mem, 2 * d)
        h32, h16 = _cross(h16.reshape(bsz, seq, d), h32.reshape(bsz, seq, d), kv, w_xq[l].astype(BF16),
                          w_xo[l].astype(BF16), ln2_g[l], ln2_b[l])
        h32, h16 = h32.reshape(t, d), h16.reshape(t, d)

        h32, h16 = _mlp(h16, h32, w_ff1[l].astype(BF16), w_ff2[l].astype(BF16), ln3_g[l], ln3_b[l])

    return h32.reshape(bsz, seq, d)
```

```python
import functools
import math

import jax
import jax.numpy as jnp
from jax import lax
from jax.experimental import pallas as pl
from jax.experimental.pallas import tpu as pltpu

F32 = jnp.float32
BF16 = jnp.bfloat16

D_MODEL = 1024
DEPTH = 2
CHUNK = 64
A_HEADS = 8
A_HEAD_DIM = 64
A_VAL_DIM = 2 * A_HEAD_DIM
ROPE_THETA = 10000.0
POOL_WINDOWS = (2, 4, 8, 16)
POOL_GROUP_DIM = D_MODEL // len(POOL_WINDOWS)
R_HEAD_DIM = 64
DECAY_LORA = 64
AAA_LORA = 64
GATE_LORA = 160
X_HEADS = 4
X_HEAD_DIM = D_MODEL // X_HEADS
DN_ALPHA = (2 * DEPTH) ** 0.25
LN_EPS = 1e-5
RMS_EPS = 1e-5
RWKV_GN_EPS = 64e-5
KK_NORM_EPS = 1e-12
NEG_INF = -1e30
LOG2_E = math.log2(math.e)
ONES_ROWS = 16

V7X_LANES = 128
V7X_MXU_DIM = 256
V7X_VMEM_LIMIT_BYTES = 56 * 1024 * 1024

ROW_TILE = 1024
COL_TILE = 1024
IN_PROJ_ROW_TILE = 512
IN_PROJ_CHUNK = 1024
FUSED_ROW_TILE = 512
MLP_ROW_TILE = 512
MLP_FF_TILE = 1024
ATTN_Q_TILE = 256
ATTN_K_TILE = 256
ATTN_HEADS_PER_STEP = 2
RW_CHUNK = 64
RW_QUAD = V7X_MXU_DIM
RW_TILE = 256
HALO_ROWS = 16
LORA_XW_PAD = V7X_LANES
LORA_XA_PAD = V7X_LANES
LORA_XG_PAD = 2 * V7X_LANES
LORA_PAD = LORA_XW_PAD + LORA_XA_PAD + LORA_XG_PAD


def _tile(n, pref):
    return pref if n % pref == 0 else n


def _params(*sem):
    return pltpu.CompilerParams(dimension_semantics=sem, vmem_limit_bytes=V7X_VMEM_LIMIT_BYTES)


def _ln(z, g, b, eps):
    mu = jnp.mean(z, -1, keepdims=True)
    d = z - mu
    var = jnp.mean(d * d, -1, keepdims=True)
    return d * lax.rsqrt(var + eps) * g + b


_DIMS = {"nn": (((1,), (0,)), ((), ())), "nt": (((1,), (1,)), ((), ())), "tn": (((0,), (0,)), ((), ()))}


def _dot(a, b):
    return jnp.dot(a, b, preferred_element_type=F32)


def _dot_nt(a, b):
    return lax.dot_general(a, b, _DIMS["nt"], preferred_element_type=F32)


def _sigmoid(x):
    return 0.5 * jnp.tanh(0.5 * x) + 0.5


def _split(x):
    hi = x.astype(BF16)
    return hi, (x - hi.astype(F32)).astype(BF16)


def _mm(a, b, kind="nn"):
    return lax.dot_general(a.astype(BF16), b.astype(BF16), _DIMS[kind], preferred_element_type=F32)


def _ln_in_kernel(x_ref, g_ref, b_ref, h32_ref, h16_ref):
    h = _ln(x_ref[...], g_ref[...], b_ref[...], LN_EPS)
    h32_ref[...] = h
    h16_ref[...] = h.astype(BF16)


def _ln_in(x2d, g, b):
    t, d = x2d.shape
    tm = _tile(t, ROW_TILE)
    row = pl.BlockSpec((tm, d), lambda i: (i, 0))
    vec = pl.BlockSpec((1, d), lambda i: (0, 0))
    return pl.pallas_call(
        _ln_in_kernel,
        grid=(t // tm,),
        in_specs=[row, vec, vec],
        out_specs=[row, row],
        out_shape=[jax.ShapeDtypeStruct((t, d), F32), jax.ShapeDtypeStruct((t, d), BF16)],
        compiler_params=_params("parallel"),
        name="ln_in",
    )(x2d, g.reshape(1, d), b.reshape(1, d))


def _proj_kernel(x_ref, w_ref, o_ref):
    o_ref[...] = _dot(x_ref[...].astype(BF16), w_ref[...]).astype(o_ref.dtype)


def _proj(x, w16, out_dtype, name):
    m, k = x.shape
    n = w16.shape[1]
    tm = _tile(m, ROW_TILE)
    tn = _tile(n, COL_TILE)
    return pl.pallas_call(
        _proj_kernel,
        grid=(m // tm, n // tn),
        in_specs=[pl.BlockSpec((tm, k), lambda i, j: (i, 0)), pl.BlockSpec((k, tn), lambda i, j: (0, j))],
        out_specs=pl.BlockSpec((tm, tn), lambda i, j: (i, j)),
        out_shape=jax.ShapeDtypeStruct((m, n), out_dtype),
        compiler_params=_params("parallel", "parallel"),
        name=name,
    )(x, w16)


def _rope(t, cos, sin, first_half):
    partner = jnp.where(first_half, pltpu.roll(t, V7X_LANES - A_HEAD_DIM // 2, 1), pltpu.roll(t, A_HEAD_DIM // 2, 1))
    return t * cos + partner * sin


def _in_proj_kernel(x_ref, halo_ref, w_ref, cos_ref, sin_ref, bias_ref, mu_ref,
                    qk_ref, v_ref, pool_ref, rkv_ref, lora_ref, gate_ref, *, tiles_per_seq):
    x = x_ref[...]
    at_start = pl.program_id(0) % tiles_per_seq == 0
    halo = halo_ref[...]
    x_ext = jnp.concatenate([jnp.where(at_start, jnp.zeros_like(halo), halo), x], axis=0)
    cos = cos_ref[...]
    sin = sin_ref[...]
    lane = lax.broadcasted_iota(jnp.int32, cos.shape, 1)
    first_half = (lane & (A_HEAD_DIM - 1)) < A_HEAD_DIM // 2
    col = 0
    mu_col = 0
    for o_ref in (qk_ref, v_ref, pool_ref, rkv_ref, lora_ref, gate_ref):
        width = A_HEADS * A_VAL_DIM if o_ref is v_ref else o_ref.shape[1]
        for c0 in range(0, width, IN_PROJ_CHUNK):
            c1 = min(c0 + IN_PROJ_CHUNK, width)
            if o_ref is rkv_ref or o_ref is lora_ref:
                t_ext = _dot(x_ext, w_ref[:, col + c0:col + c1])
                t = t_ext[HALO_ROWS:]
                prev = pltpu.roll(t_ext, 1, 0)[HALO_ROWS:]
                mu = mu_ref[:, mu_col + c0:mu_col + c1]
                o_ref[:, c0:c1] = (t + (prev - t) * mu).astype(o_ref.dtype)
                continue
            t = _dot(x, w_ref[:, col + c0:col + c1])
            if o_ref is qk_ref:
                for j in range(c0, c1, V7X_LANES):
                    blk = t[:, j - c0:j - c0 + V7X_LANES]
                    o_ref[:, j:j + V7X_LANES] = _rope(blk, cos, sin, first_half).astype(o_ref.dtype)
            elif o_ref is gate_ref:
                o_ref[:, c0:c1] = jax.nn.sigmoid(t + bias_ref[:, c0:c1]).astype(o_ref.dtype)
            elif o_ref is v_ref:
                tk = v_ref.shape[4]
                for h in range((c1 - c0) // A_VAL_DIM):
                    for kt in range(v_ref.shape[2]):
                        blk = t[kt * tk:(kt + 1) * tk, h * A_VAL_DIM:(h + 1) * A_VAL_DIM]
                        head = c0 // A_VAL_DIM + h
                        v_ref[0, head, kt, :A_VAL_DIM, :] = blk.T.astype(v_ref.dtype)
                        v_ref[0, head, kt, A_VAL_DIM:, :] = jnp.ones((ONES_ROWS, tk), v_ref.dtype)
            else:
                o_ref[:, c0:c1] = t.astype(o_ref.dtype)
        col += width
        if o_ref is rkv_ref or o_ref is lora_ref:
            mu_col += width


def _in_proj(x16, w16, cos_t, sin_t, gate_bias, shift_mu, widths, seq, tk):
    m, k = x16.shape
    tm = _tile(seq, IN_PROJ_ROW_TILE)
    tiles_per_seq = seq // tm
    row = lambda n: pl.BlockSpec((tm, n), lambda i: (i, 0))
    vt_rows = A_VAL_DIM + ONES_ROWS
    vt_spec = pl.BlockSpec((1, A_HEADS, tm // tk, vt_rows, tk),
                           lambda i: (i // tiles_per_seq, 0, i % tiles_per_seq, 0, 0))
    out_specs = [row(n) for n in widths]
    out_shape = [jax.ShapeDtypeStruct((m, n), BF16) for n in widths]
    out_specs[1] = vt_spec
    out_shape[1] = jax.ShapeDtypeStruct((m // seq, A_HEADS, seq // tk, vt_rows, tk), BF16)
    halo_blocks = tm // HALO_ROWS
    halo = pl.BlockSpec((HALO_ROWS, k), lambda i: (jnp.maximum(i * halo_blocks - 1, 0), 0))
    const = lambda a: pl.BlockSpec(a.shape, lambda i: (0, 0))
    return pl.pallas_call(
        functools.partial(_in_proj_kernel, tiles_per_seq=tiles_per_seq),
        grid=(m // tm,),
        in_specs=[row(k), halo, pl.BlockSpec(w16.shape, lambda i: (0, 0), pipeline_mode=pl.Buffered(1)),
                  row(V7X_LANES), row(V7X_LANES), const(gate_bias), const(shift_mu)],
        out_specs=out_specs,
        out_shape=out_shape,
        compiler_params=_params("parallel"),
        name="in_proj",
    )(x16, x16, w16, cos_t, sin_t, gate_bias, shift_mu)


def _diff_attn_kernel(lam_ref, g_ref, q_ref, k_ref, v_ref, o_ref, s_buf, p_buf, *, lam_init):
    i = pl.program_id(2)
    tq = q_ref.shape[1]
    tk = s_buf.shape[2]
    n_heads = s_buf.shape[0]
    head_cols = [slice(h * A_VAL_DIM, (h + 1) * A_VAL_DIM) for h in range(n_heads)]

    def stacked_queries(cols):
        q = (q_ref[0, :, cols].astype(F32) * (A_HEAD_DIM ** -0.5 * LOG2_E)).astype(BF16)
        lane = lax.broadcasted_iota(jnp.int32, q.shape, 1)
        zero = jnp.zeros_like(q)
        return jnp.concatenate([jnp.where(lane < A_HEAD_DIM, q, zero), jnp.where(lane >= A_HEAD_DIM, q, zero)], axis=0)

    qs = [stacked_queries(cols) for cols in head_cols]

    def rows(j):
        return pl.ds(pl.multiple_of(j * tk, tk), tk)

    def next_scores(h, j):
        return _dot_nt(k_ref[0, rows(j), head_cols[h]], qs[h])

    def values_product(h, j, p):
        return _dot(v_ref[0, h, j], p)

    def softmax_step(j, m, s, masked):
        if masked:
            q_off = lax.broadcasted_iota(jnp.int32, (1, 2 * tq), 1) & (tq - 1)
            delta = (i * tq - j * tk) // CHUNK
            s = jnp.concatenate([jnp.where(q_off >= (a - delta) * CHUNK, s[a * CHUNK:(a + 1) * CHUNK], NEG_INF)
                                 for a in range(tk // CHUNK)], axis=0)
        m_new = jnp.maximum(m, jnp.max(s, axis=0, keepdims=True))
        return m_new, jnp.exp2(m - m_new), jnp.exp2(s - m_new).astype(BF16)

    def body(j, carries):
        slot = j & 1
        loaded = [(s_buf[h, slot], p_buf[h, 1 - slot]) for h in range(n_heads)]
        pvs = [values_product(h, jnp.maximum(j - 1, 0), loaded[h][1]) for h in range(n_heads)]
        s_next = [next_scores(h, j + 1) for h in range(n_heads)]
        stepped = [softmax_step(j, carries[h][0], loaded[h][0], False) for h in range(n_heads)]
        for h, (_, _, p) in enumerate(stepped):
            s_buf[h, 1 - slot] = s_next[h]
            p_buf[h, slot] = p
        return tuple((m_new, alpha * (carries[h][1] + pvs[h])) for h, (m_new, alpha, _) in enumerate(stepped))

    for h in range(n_heads):
        s_buf[h, 0] = next_scores(h, 0)
        p_buf[h, 1] = jnp.zeros(p_buf.shape[2:], BF16)
    init = (jnp.full((1, 2 * tq), NEG_INF, F32), jnp.zeros((v_ref.shape[3], 2 * tq), F32))
    n_full = (i * tq) // tk
    carries = lax.fori_loop(0, n_full, body, (init,) * n_heads)

    lam_rows = lam_ref[...]
    lam = (jnp.exp(jnp.sum(lam_rows[0:1] * lam_rows[1:2], -1, keepdims=True))
           - jnp.exp(jnp.sum(lam_rows[2:3] * lam_rows[3:4], -1, keepdims=True)) + lam_init)
    last = n_full & 1
    for h in range(n_heads):
        pv = values_product(h, jnp.maximum(n_full - 1, 0), p_buf[h, 1 - last])
        _, alpha, p = softmax_step(n_full, carries[h][0], s_buf[h, last], True)
        acc = alpha * (carries[h][1] + pv) + values_product(h, n_full, p)
        out = acc[:A_VAL_DIM] / acc[A_VAL_DIM:A_VAL_DIM + 1]
        o = out[:, :tq] - lam * out[:, tq:]
        o = o * lax.rsqrt(jnp.mean(o * o, axis=0, keepdims=True) + RMS_EPS) * g_ref[...]
        o_ref[0, :, head_cols[h]] = (o * (1.0 - lam_init)).T.astype(o_ref.dtype)


def _diff_attn(qk, vt, lam_rows, subln_g, lam_init):
    b, _, n_ktiles, vt_rows, tk = vt.shape
    s = n_ktiles * tk
    tq = _tile(s, ATTN_Q_TILE)
    nh = ATTN_HEADS_PER_STEP
    width = nh * A_VAL_DIM
    k_blk0 = A_HEADS // nh
    return pl.pallas_call(
        functools.partial(_diff_attn_kernel, lam_init=lam_init),
        grid=(b, A_HEADS // nh, s // tq),
        in_specs=[
            pl.BlockSpec((4, A_HEAD_DIM), lambda bi, h, i: (0, 0)),
            pl.BlockSpec((A_VAL_DIM, 1), lambda bi, h, i: (0, 0)),
            pl.BlockSpec((1, tq, width), lambda bi, h, i: (bi, i, h)),
            pl.BlockSpec((1, s, width), lambda bi, h, i: (bi, 0, k_blk0 + h)),
            pl.BlockSpec((1, nh, n_ktiles, vt_rows, tk), lambda bi, h, i: (bi, h, 0, 0, 0)),
        ],
        out_specs=pl.BlockSpec((1, tq, width), lambda bi, h, i: (bi, i, h)),
        out_shape=jax.ShapeDtypeStruct((b, s, A_HEADS * A_VAL_DIM), BF16),
        scratch_shapes=[pltpu.VMEM((nh, 2, tk, 2 * tq), F32), pltpu.VMEM((nh, 2, tk, 2 * tq), BF16)],
        compiler_params=_params("parallel", "parallel", "arbitrary"),
        name="diff_attn",
    )(lam_rows, subln_g.reshape(A_VAL_DIM, 1), qk, qk, vt)


def _pool_kernel(p_ref, w_ref, scale_ref, o_ref):
    g = pl.program_id(1)
    x = p_ref[0].astype(F32)
    row = lax.broadcasted_iota(jnp.int32, x.shape, 0)
    for gi, window in enumerate(POOL_WINDOWS):
        @pl.when(g == gi)
        def _(window=window):
            ws = x
            shift = 1
            while shift < window:
                ws = ws + jnp.where(row >= shift, pltpu.roll(ws, shift, 0), 0.0)
                shift *= 2
            count = jnp.minimum(row + 1, window).astype(F32)
            d = ws / count - x
            y = _dot(d.astype(BF16), w_ref[0]) * scale_ref[...]
            o_ref[0] = y.astype(o_ref.dtype)


def _pool(u_pool, pool_w16, pool_scale):
    b, s, d = u_pool.shape
    cg = POOL_GROUP_DIM
    return pl.pallas_call(
        _pool_kernel,
        grid=(b, len(POOL_WINDOWS)),
        in_specs=[
            pl.BlockSpec((1, s, cg), lambda bi, g: (bi, 0, g)),
            pl.BlockSpec((1, cg, cg), lambda bi, g: (g, 0, 0)),
            pl.BlockSpec((1, cg), lambda bi, g: (0, g)),
        ],
        out_specs=pl.BlockSpec((1, s, cg), lambda bi, g: (bi, 0, g)),
        out_shape=jax.ShapeDtypeStruct((b, s, d), BF16),
        compiler_params=_params("parallel", "parallel"),
        name="pool_mixer",
    )(u_pool, pool_w16, pool_scale.reshape(1, d))


def _block_mask():
    r = lax.broadcasted_iota(jnp.int32, (RW_QUAD, RW_QUAD), 0)
    c = lax.broadcasted_iota(jnp.int32, (RW_QUAD, RW_QUAD), 1)
    return (r // R_HEAD_DIM) == (c // R_HEAD_DIM)


def _bd(x, mask):
    tiled = jnp.concatenate([x] * (RW_QUAD // R_HEAD_DIM), axis=0)
    return jnp.where(mask, tiled, jnp.zeros_like(tiled))


def _seg_quad(x, ones_bd):
    hi, lo = _split(x)
    return _dot(hi, ones_bd) + _dot(lo, ones_bd)


def _seg_sum(x, ones_bd):
    parts = [_seg_quad(x[:, q * RW_QUAD:(q + 1) * RW_QUAD], ones_bd) for q in range(x.shape[1] // RW_QUAD)]
    return jnp.concatenate(parts, axis=1)


def _chunk_summaries(units, mask):
    c = RW_CHUNK
    ii = lax.broadcasted_iota(jnp.int32, (c, RW_QUAD), 0)
    jj = lax.broadcasted_iota(jnp.int32, (c, RW_QUAD), 1) % c
    strict = ii > jj
    incl = ii >= jj
    eye = (ii == jj).astype(F32)
    bd = lambda x: _bd(x, mask)

    scaled = []
    for r, k, v, a, b, lw, cum in units:
        cum_end = cum[c - 1:c, :]
        at = a * jnp.exp(cum - lw)
        rt = r * jnp.exp(cum)
        e_neg = jnp.exp(-cum)
        w_end = jnp.exp(cum_end)
        bt = b * e_neg
        kt = k * e_neg
        scaled.append((at, rt, bt, kt, bt * w_end, kt * w_end, v, w_end))

    scores = [_mm(jnp.concatenate([at, rt], axis=0), jnp.concatenate([bd(bt), bd(kt)], axis=0), "nt")
              for at, rt, bt, kt, _, _, _, _ in scaled]
    a_ab = [jnp.where(strict, sc[:c, :RW_QUAD], 0.0) for sc in scores]
    a_rb = [jnp.where(incl, sc[c:, :RW_QUAD], 0.0) for sc in scores]
    av = [_mm(jnp.concatenate([jnp.where(strict, sc[:c, RW_QUAD:], 0.0), jnp.where(incl, sc[c:, RW_QUAD:], 0.0)],
                              axis=0), bd(u[6])) for sc, u in zip(scores, scaled)]

    tinv = [a + eye for a in a_ab]
    pk = [_mm(a, bd(a)) for a in a_ab]
    for _ in range(int(math.log2(c)) - 2):
        both = [_mm(jnp.concatenate([p, t], axis=0), bd(p)) for p, t in zip(pk, tinv)]
        tinv = [t + b[c:] for t, b in zip(tinv, both)]
        pk = [b[:c] for b in both]
    tinv = [t + _mm(t, bd(p)) for t, p in zip(tinv, pk)]

    side = lambda x, y: jnp.concatenate([x, y], axis=1)
    out = []
    for t, arb, avu, (at, rt, _, _, bh, kh, v, w_end) in zip(tinv, a_rb, av, scaled):
        x = _mm(t, side(bd(at), bd(avu[:c])))
        xa, xu = x[:, :RW_QUAD], x[:, RW_QUAD:]
        ry = _mm(arb, side(bd(xa), bd(xu)))
        z = _mm(bh, x, "tn")
        z1 = jnp.where(mask, z[:, :RW_QUAD], 0.0)
        z2 = jnp.where(mask, z[:, RW_QUAD:] + _mm(kh, v, "tn"), 0.0)
        fold = lambda z: functools.reduce(jnp.add, [z[h * c:(h + 1) * c] for h in range(RW_QUAD // R_HEAD_DIM)])
        out.append((rt + ry[:, :RW_QUAD], avu[c:] + ry[:, RW_QUAD:], fold(z1) + eye * w_end, fold(z2)))
    return out


def _rwkv_kernel(rkv_ref, lora_ref, vec_ref, w2_ref, a2_ref, g2_ref, o_ref, state_ref):
    ts = rkv_ref.shape[1]
    d = D_MODEL

    @pl.when(pl.program_id(1) == 0)
    def _():
        state_ref[...] = jnp.zeros_like(state_ref)

    xs = rkv_ref[0].astype(F32)
    ls = lora_ref[0].astype(F32)
    r = xs[:, :d]
    k = xs[:, d:2 * d]
    v = xs[:, 2 * d:]
    vec = vec_ref[...]
    w0, a0, k_k, k_a, r_k = vec[0:1], vec[1:2], vec[2:3], vec[3:4], vec[4:5]

    xa0 = LORA_XW_PAD
    xg0 = LORA_XW_PAD + LORA_XA_PAD
    z = w0 + _dot(jnp.tanh(ls[:, :xa0]).astype(BF16), w2_ref[...])
    lw = -math.exp(-0.5) * _sigmoid(z)
    a_sig = _sigmoid(a0 + _dot(ls[:, xa0:xg0].astype(BF16), a2_ref[...]))
    gate = _dot(_sigmoid(ls[:, xg0:]).astype(BF16), g2_ref[...])

    mask = _block_mask()
    ones_bd = mask.astype(BF16)
    kk = k * k_k
    kk = kk * lax.rsqrt(jnp.maximum(_seg_sum(kk * kk, ones_bd), KK_NORM_EPS * KK_NORM_EPS))
    k2 = k * (1.0 + (a_sig - 1.0) * k_a)
    a_vec = -kk
    b_vec = kk * a_sig
    bonus = _seg_sum(r * k2 * r_k, ones_bd) * v

    ti = lax.broadcasted_iota(jnp.int32, (ts, ts), 0)
    tj = lax.broadcasted_iota(jnp.int32, (ts, ts), 1)
    tri = ((tj <= ti) & (tj // RW_CHUNK == ti // RW_CHUNK)).astype(BF16)
    lw_hi, lw_lo = _split(lw)
    cum = _dot(tri, lw_hi) + _dot(tri, lw_lo)

    windows = [(slice(c * RW_CHUNK, (c + 1) * RW_CHUNK), slice(q * RW_QUAD, (q + 1) * RW_QUAD))
               for c in range(ts // RW_CHUNK) for q in range(d // RW_QUAD)]
    units = [tuple(x[rows, cols] for x in (r, k2, v, a_vec, b_vec, lw, cum)) for rows, cols in windows]
    summaries = _chunk_summaries(units, mask)

    inv_n = 1.0 / R_HEAD_DIM
    lnx_g, lnx_b = vec[5:6], vec[6:7]
    n_quads = d // RW_QUAD
    states = [state_ref[:, q * RW_QUAD:(q + 1) * RW_QUAD] for q in range(n_quads)]
    start_states = []
    for (rows, cols), (rp, y0, m, n) in zip(windows, summaries):
        q = cols.start // RW_QUAD
        start_bd = _bd(states[q], mask)
        start_states.append(start_bd)
        states[q] = _mm(m, start_bd) + n
    for q in range(n_quads):
        state_ref[:, q * RW_QUAD:(q + 1) * RW_QUAD] = states[q]

    ys = [_mm(rp, h0) + y0 for (rp, y0, _, _), h0 in zip(summaries, start_states)]
    mus = [_seg_quad(y, ones_bd) * inv_n for y in ys]
    dys = [y - mu for y, mu in zip(ys, mus)]
    variances = [_seg_quad(dy * dy, ones_bd) * inv_n for dy in dys]
    for (rows, cols), dy, var in zip(windows, dys, variances):
        yn = dy * lax.rsqrt(var + RWKV_GN_EPS) * lnx_g[:, cols] + lnx_b[:, cols]
        o_ref[0, rows, cols] = ((yn + bonus[rows, cols]) * gate[rows, cols]).astype(o_ref.dtype)


def _rwkv(u_rkv, u_lora, vecs, w2p, a2p, g2p):
    b, s, w3 = u_rkv.shape
    d = D_MODEL
    ts = _tile(s, RW_TILE)

    def tile_spec(width):
        return pl.BlockSpec((1, ts, width), lambda bi, t: (bi, t, 0))

    def const_spec(shape):
        return pl.BlockSpec(shape, lambda bi, t: (0,) * len(shape))

    return pl.pallas_call(
        _rwkv_kernel,
        grid=(b, s // ts),
        in_specs=[tile_spec(w3), tile_spec(LORA_PAD), const_spec(vecs.shape),
                  const_spec(w2p.shape), const_spec(a2p.shape), const_spec(g2p.shape)],
        out_specs=tile_spec(d),
        out_shape=jax.ShapeDtypeStruct((b, s, d), BF16),
        scratch_shapes=[pltpu.VMEM((R_HEAD_DIM, d), F32)],
        compiler_params=_params("parallel", "arbitrary"),
        name="rwkv7",
    )(u_rkv, u_lora, vecs, w2p, a2p, g2p)


def _merge_kernel(ya_ref, yb_ref, yc_ref, gates_ref, h_ref, wa_ref, wb_ref, wc_ref, wo_ref,
                  g_ref, b_ref, h32_ref, h16_ref):
    d = D_MODEL
    merged = jnp.zeros((ya_ref.shape[0], d), F32)
    for idx, (y_ref, w_ref) in enumerate(((ya_ref, wa_ref), (yb_ref, wb_ref), (yc_ref, wc_ref))):
        gate = gates_ref[:, idx * d:(idx + 1) * d].astype(F32)
        merged = merged + gate * _dot(y_ref[...], w_ref[...])
    z = DN_ALPHA * h_ref[...] + _dot(merged.astype(BF16), wo_ref[...])
    h = _ln(z, g_ref[...], b_ref[...], LN_EPS)
    h32_ref[...] = h
    h16_ref[...] = h.astype(BF16)


def _merge(ya, yb, yc, gates, h32, wa, wb, wc, wo, ln_g, ln_b):
    t, d = h32.shape
    tm = _tile(t, FUSED_ROW_TILE)
    row = pl.BlockSpec((tm, d), lambda i: (i, 0))
    wspec = pl.BlockSpec((d, d), lambda i: (0, 0))
    vec = pl.BlockSpec((1, d), lambda i: (0, 0))
    return pl.pallas_call(
        _merge_kernel,
        grid=(t // tm,),
        in_specs=[row, row, row, pl.BlockSpec((tm, 3 * d), lambda i: (i, 0)),
                  row, wspec, wspec, wspec, wspec, vec, vec],
        out_specs=[row, row],
        out_shape=[jax.ShapeDtypeStruct((t, d), F32), jax.ShapeDtypeStruct((t, d), BF16)],
        compiler_params=_params("parallel"),
        name="merge_out",
    )(ya, yb, yc, gates, h32, wa, wb, wc, wo, ln_g.reshape(1, d), ln_b.reshape(1, d))


def _cross_kernel(h16_ref, h32_ref, kv_ref, wq_ref, wo_ref, g_ref, b_ref, o32_ref, o16_ref):
    d = D_MODEL
    q = _dot(h16_ref[0], wq_ref[...]).astype(BF16)
    kv = kv_ref[0]
    head_cols = [slice(hd * X_HEAD_DIM, (hd + 1) * X_HEAD_DIM) for hd in range(X_HEADS)]
    scores = [_dot_nt(q[:, cols], kv[:, cols]) * (X_HEAD_DIM ** -0.5) for cols in head_cols]
    probs = [jnp.exp(s - jnp.max(s, -1, keepdims=True)) for s in scores]
    outs = [_dot(p.astype(BF16), kv[:, d + cols.start:d + cols.stop]) for p, cols in zip(probs, head_cols)]
    attn = jnp.concatenate([(o / jnp.sum(p, -1, keepdims=True)).astype(BF16) for o, p in zip(outs, probs)], axis=1)
    z = DN_ALPHA * h32_ref[0] + _dot(attn, wo_ref[...])
    h = _ln(z, g_ref[...], b_ref[...], LN_EPS)
    o32_ref[0] = h
    o16_ref[0] = h.astype(BF16)


def _cross(h16, h32, kv, wq, wo, ln_g, ln_b):
    b, s, d = h32.shape
    tm = _tile(s, FUSED_ROW_TILE)
    row = pl.BlockSpec((1, tm, d), lambda bi, i: (bi, i, 0))
    wspec = pl.BlockSpec((d, d), lambda bi, i: (0, 0))
    vec = pl.BlockSpec((1, d), lambda bi, i: (0, 0))
    n_mem = kv.shape[1]
    return pl.pallas_call(
        _cross_kernel,
        grid=(b, s // tm),
        in_specs=[row, row, pl.BlockSpec((1, n_mem, 2 * d), lambda bi, i: (bi, 0, 0)), wspec, wspec, vec, vec],
        out_specs=[row, row],
        out_shape=[jax.ShapeDtypeStruct((b, s, d), F32), jax.ShapeDtypeStruct((b, s, d), BF16)],
        compiler_params=_params("parallel", "parallel"),
        name="cross_attn",
    )(h16, h32, kv, wq, wo, ln_g.reshape(1, d), ln_b.reshape(1, d))


def _mlp_kernel(h16_ref, h32_ref, w1_ref, w2_ref, g_ref, b_ref, o32_ref, o16_ref):
    x = h16_ref[...]
    dff = w1_ref.shape[1]
    tf = min(MLP_FF_TILE, dff)
    acc = DN_ALPHA * h32_ref[...]
    for f in range(dff // tf):
        a = jnp.maximum(_dot(x, w1_ref[:, f * tf:(f + 1) * tf]), 0.0)
        acc = acc + _dot((a * a).astype(BF16), w2_ref[f * tf:(f + 1) * tf, :])
    h = _ln(acc, g_ref[...], b_ref[...], LN_EPS)
    o32_ref[...] = h
    o16_ref[...] = h.astype(BF16)


def _mlp(h16, h32, w1, w2, ln_g, ln_b):
    t, d = h32.shape
    dff = w1.shape[1]
    tm = _tile(t, MLP_ROW_TILE)
    row = pl.BlockSpec((tm, d), lambda i: (i, 0))
    vec = pl.BlockSpec((1, d), lambda i: (0, 0))
    resident = pl.Buffered(1)
    return pl.pallas_call(
        _mlp_kernel,
        grid=(t // tm,),
        in_specs=[row, row, pl.BlockSpec((d, dff), lambda i: (0, 0), pipeline_mode=resident),
                  pl.BlockSpec((dff, d), lambda i: (0, 0), pipeline_mode=resident), vec, vec],
        out_specs=[row, row],
        out_shape=[jax.ShapeDtypeStruct((t, d), F32), jax.ShapeDtypeStruct((t, d), BF16)],
        compiler_params=_params("parallel"),
        name="mlp",
    )(h16, h32, w1, w2, ln_g.reshape(1, d), ln_b.reshape(1, d))


def _pad_rows(w, rows):
    return jnp.pad(w, ((0, rows - w.shape[0]), (0, 0)))


def _lora_layout(w):
    pad = lambda t, n: jnp.pad(t, [(0, 0)] * (t.ndim - 1) + [(0, n - t.shape[-1])])
    xw = w[..., :DECAY_LORA]
    xa = w[..., DECAY_LORA:DECAY_LORA + AAA_LORA]
    xg = w[..., DECAY_LORA + AAA_LORA:]
    return jnp.concatenate([pad(xw, LORA_XW_PAD), pad(xa, LORA_XA_PAD), pad(xg, LORA_XG_PAD)], axis=-1)


def kernel(x, mem, positions, ln_in_g, ln_in_b, w_in, b_gate, lam_q1, lam_k1, lam_q2, lam_k2, attn_subln_g, w_br_attn, pool_w, pool_scale, w_br_pool, rwkv_mu, rwkv_w0, rwkv_w2, rwkv_a0, rwkv_a2, rwkv_g2, rwkv_k_k, rwkv_k_a, rwkv_r_k, rwkv_lnx_g, rwkv_lnx_b, w_br_rwkv, w_out, ln1_g, ln1_b, w_xq, w_xkv, w_xo, ln2_g, ln2_b, w_ff1, w_ff2, ln3_g, ln3_b):
    bsz, seq, d = x.shape
    t = bsz * seq
    n_mem = mem.shape[1]
    depth = w_in.shape[0]

    inv_freq = 1.0 / (ROPE_THETA ** (jnp.arange(0, A_HEAD_DIM, 2, dtype=F32) / A_HEAD_DIM))
    ang = positions.astype(F32).reshape(t, 1) * inv_freq
    cos, sin = jnp.cos(ang), jnp.sin(ang)
    cos_t = jnp.concatenate([cos, cos, cos, cos], axis=-1)
    sin_t = jnp.concatenate([-sin, sin, -sin, sin], axis=-1)

    c_qk = 2 * A_HEADS * 2 * A_HEAD_DIM
    c_v = c_qk + A_HEADS * A_VAL_DIM
    c_pool = c_v + d
    c_rkv = c_pool + 3 * d
    c_lora = c_rkv + DECAY_LORA + AAA_LORA + GATE_LORA

    h32, h16 = _ln_in(x.reshape(t, d), ln_in_g, ln_in_b)
    mem2d = mem.reshape(bsz * n_mem, d)

    for l in range(depth):
        w = w_in[l]
        w16 = jnp.concatenate([w[:, :c_rkv], _lora_layout(w[:, c_rkv:c_lora]), w[:, c_lora:]], axis=1).astype(BF16)
        widths = (c_qk, c_v - c_qk, c_pool - c_v, c_rkv - c_pool, LORA_PAD, w.shape[1] - c_lora)
        mu = rwkv_mu[l]
        shift_mu = jnp.concatenate([mu[:3 * d], _lora_layout(mu[3 * d:])]).reshape(1, 3 * d + LORA_PAD)
        qk, vt, u_pool, u_rkv, u_lora, gates = _in_proj(h16, w16, cos_t, sin_t, b_gate[l].reshape(1, 3 * d), shift_mu,
                                                        widths, seq, min(ATTN_K_TILE, seq))
        qk = qk.reshape(bsz, seq, c_qk)
        u_pool = u_pool.reshape(bsz, seq, d)
        u_rkv = u_rkv.reshape(bsz, seq, 3 * d)
        u_lora = u_lora.reshape(bsz, seq, LORA_PAD)

        lam_init = 0.8 - 0.6 * math.exp(-0.3 * l)
        lam_rows = jnp.stack([lam_q1[l], lam_k1[l], lam_q2[l], lam_k2[l]]).astype(F32)
        y_a = _diff_attn(qk, vt, lam_rows, attn_subln_g[l], lam_init)

        y_b = _pool(u_pool, pool_w[l].astype(BF16), pool_scale[l])

        vecs = jnp.stack([rwkv_w0[l], rwkv_a0[l], rwkv_k_k[l], rwkv_k_a[l], rwkv_r_k[l].reshape(d),
                          rwkv_lnx_g[l], rwkv_lnx_b[l], jnp.zeros((d,), F32)])
        w2p = _pad_rows(rwkv_w2[l], LORA_XW_PAD).astype(BF16)
        a2p = _pad_rows(rwkv_a2[l], LORA_XA_PAD).astype(BF16)
        g2p = _pad_rows(rwkv_g2[l], LORA_XG_PAD).astype(BF16)
        y_c = _rwkv(u_rkv, u_lora, vecs, w2p, a2p, g2p)

        h32, h16 = _merge(y_a.reshape(t, d), y_b.reshape(t, d), y_c.reshape(t, d), gates, h32,
                          w_br_attn[l].astype(BF16), w_br_pool[l].astype(BF16), w_br_rwkv[l].astype(BF16),
                          w_out[l].astype(BF16), ln1_g[l], ln1_b[l])

        kv = _proj(mem2d, w_xkv[l].astype(BF16), BF16, name="proj_kv").reshape(bsz, n_mem, 2 * d)
        h32, h16 = _cross(h16.reshape(bsz, seq, d), h32.reshape(bsz, seq, d), kv, w_xq[l].astype(BF16),
                          w_xo[l].astype(BF16), ln2_g[l], ln2_b[l])
        h32, h16 = h32.reshape(t, d), h16.reshape(t, d)

        h32, h16 = _mlp(h16, h32, w_ff1[l].astype(BF16), w_ff2[l].astype(BF16), ln3_g[l], ln3_b[l])

    return h32.reshape(bsz, seq, d)
```

```python
import functools
import math

import jax
import jax.numpy as jnp
from jax import lax
from jax.experimental import pallas as pl
from jax.experimental.pallas import tpu as pltpu

F32 = jnp.float32
BF16 = jnp.bfloat16

D_MODEL = 1024
DEPTH = 2
CHUNK = 64
A_HEADS = 8
A_HEAD_DIM = 64
A_VAL_DIM = 2 * A_HEAD_DIM
ROPE_THETA = 10000.0
POOL_WINDOWS = (2, 4, 8, 16)
POOL_GROUP_DIM = D_MODEL // len(POOL_WINDOWS)
R_HEAD_DIM = 64
DECAY_LORA = 64
AAA_LORA = 64
GATE_LORA = 160
X_HEADS = 4
X_HEAD_DIM = D_MODEL // X_HEADS
DN_ALPHA = (2 * DEPTH) ** 0.25
LN_EPS = 1e-5
RMS_EPS = 1e-5
RWKV_GN_EPS = 64e-5
KK_NORM_EPS = 1e-12
NEG_INF = -1e30
LOG2_E = math.log2(math.e)
ONES_ROWS = 16

V7X_LANES = 128
V7X_MXU_DIM = 256
V7X_VMEM_LIMIT_BYTES = 56 * 1024 * 1024

ROW_TILE = 1024
COL_TILE = 1024
IN_PROJ_ROW_TILE = 512
IN_PROJ_CHUNK = 1024
FUSED_ROW_TILE = 512
MLP_ROW_TILE = 512
MLP_FF_TILE = 1024
ATTN_Q_TILE = 256
ATTN_K_TILE = 256
ATTN_HEADS_PER_STEP = 2
RW_CHUNK = 64
RW_QUAD = V7X_MXU_DIM
RW_TILE = 256
HALO_ROWS = 16
LORA_XW_PAD = V7X_LANES
LORA_XA_PAD = V7X_LANES
LORA_XG_PAD = 2 * V7X_LANES
LORA_PAD = LORA_XW_PAD + LORA_XA_PAD + LORA_XG_PAD


def _tile(n, pref):
    return pref if n % pref == 0 else n


def _params(*sem):
    return pltpu.CompilerParams(dimension_semantics=sem, vmem_limit_bytes=V7X_VMEM_LIMIT_BYTES)


def _ln(z, g, b, eps):
    mu = jnp.mean(z, -1, keepdims=True)
    d = z - mu
    var = jnp.mean(d * d, -1, keepdims=True)
    return d * lax.rsqrt(var + eps) * g + b


_DIMS = {"nn": (((1,), (0,)), ((), ())), "nt": (((1,), (1,)), ((), ())), "tn": (((0,), (0,)), ((), ()))}


def _dot(a, b):
    return jnp.dot(a, b, preferred_element_type=F32)


def _dot_nt(a, b):
    return lax.dot_general(a, b, _DIMS["nt"], preferred_element_type=F32)


def _sigmoid(x):
    return 0.5 * jnp.tanh(0.5 * x) + 0.5


def _split(x):
    hi = x.astype(BF16)
    return hi, (x - hi.astype(F32)).astype(BF16)


def _mm(a, b, kind="nn"):
    return lax.dot_general(a.astype(BF16), b.astype(BF16), _DIMS[kind], preferred_element_type=F32)


def _ln_in_kernel(x_ref, g_ref, b_ref, h32_ref, h16_ref):
    h = _ln(x_ref[...], g_ref[...], b_ref[...], LN_EPS)
    h32_ref[...] = h
    h16_ref[...] = h.astype(BF16)


def _ln_in(x2d, g, b):
    t, d = x2d.shape
    tm = _tile(t, ROW_TILE)
    row = pl.BlockSpec((tm, d), lambda i: (i, 0))
    vec = pl.BlockSpec((1, d), lambda i: (0, 0))
    return pl.pallas_call(
        _ln_in_kernel,
        grid=(t // tm,),
        in_specs=[row, vec, vec],
        out_specs=[row, row],
        out_shape=[jax.ShapeDtypeStruct((t, d), F32), jax.ShapeDtypeStruct((t, d), BF16)],
        compiler_params=_params("parallel"),
        name="ln_in",
    )(x2d, g.reshape(1, d), b.reshape(1, d))


def _proj_kernel(x_ref, w_ref, o_ref):
    o_ref[...] = _dot(x_ref[...].astype(BF16), w_ref[...]).astype(o_ref.dtype)


def _proj(x, w16, out_dtype, name):
    m, k = x.shape
    n = w16.shape[1]
    tm = _tile(m, ROW_TILE)
    tn = _tile(n, COL_TILE)
    return pl.pallas_call(
        _proj_kernel,
        grid=(m // tm, n // tn),
        in_specs=[pl.BlockSpec((tm, k), lambda i, j: (i, 0)), pl.BlockSpec((k, tn), lambda i, j: (0, j))],
        out_specs=pl.BlockSpec((tm, tn), lambda i, j: (i, j)),
        out_shape=jax.ShapeDtypeStruct((m, n), out_dtype),
        compiler_params=_params("parallel", "parallel"),
        name=name,
    )(x, w16)


def _rope(t, cos, sin, first_half):
    partner = jnp.where(first_half, pltpu.roll(t, V7X_LANES - A_HEAD_DIM // 2, 1), pltpu.roll(t, A_HEAD_DIM // 2, 1))
    return t * cos + partner * sin


def _in_proj_kernel(x_ref, halo_ref, w_ref, cos_ref, sin_ref, bias_ref, mu_ref,
                    qk_ref, v_ref, pool_ref, rkv_ref, lora_ref, gate_ref, *, tiles_per_seq):
    x = x_ref[...]
    at_start = pl.program_id(0) % tiles_per_seq == 0
    halo = halo_ref[...]
    x_ext = jnp.concatenate([jnp.where(at_start, jnp.zeros_like(halo), halo), x], axis=0)
    cos = cos_ref[...]
    sin = sin_ref[...]
    lane = lax.broadcasted_iota(jnp.int32, cos.shape, 1)
    first_half = (lane & (A_HEAD_DIM - 1)) < A_HEAD_DIM // 2
    col = 0
    mu_col = 0
    for o_ref in (qk_ref, v_ref, pool_ref, rkv_ref, lora_ref, gate_ref):
        width = A_HEADS * A_VAL_DIM if o_ref is v_ref else o_ref.shape[1]
        for c0 in range(0, width, IN_PROJ_CHUNK):
            c1 = min(c0 + IN_PROJ_CHUNK, width)
            if o_ref is rkv_ref or o_ref is lora_ref:
                t_ext = _dot(x_ext, w_ref[:, col + c0:col + c1])
                t = t_ext[HALO_ROWS:]
                prev = pltpu.roll(t_ext, 1, 0)[HALO_ROWS:]
                mu = mu_ref[:, mu_col + c0:mu_col + c1]
                o_ref[:, c0:c1] = (t + (prev - t) * mu).astype(o_ref.dtype)
                continue
            t = _dot(x, w_ref[:, col + c0:col + c1])
            if o_ref is qk_ref:
                for j in range(c0, c1, V7X_LANES):
                    blk = t[:, j - c0:j - c0 + V7X_LANES]
                    o_ref[:, j:j + V7X_LANES] = _rope(blk, cos, sin, first_half).astype(o_ref.dtype)
            elif o_ref is gate_ref:
                o_ref[:, c0:c1] = jax.nn.sigmoid(t + bias_ref[:, c0:c1]).astype(o_ref.dtype)
            elif o_ref is v_ref:
                tk = v_ref.shape[4]
                for h in range((c1 - c0) // A_VAL_DIM):
                    for kt in range(v_ref.shape[2]):
                        blk = t[kt * tk:(kt + 1) * tk, h * A_VAL_DIM:(h + 1) * A_VAL_DIM]
                        head = c0 // A_VAL_DIM + h
                        v_ref[0, head, kt, :A_VAL_DIM, :] = blk.T.astype(v_ref.dtype)
                        v_ref[0, head, kt, A_VAL_DIM:, :] = jnp.ones((ONES_ROWS, tk), v_ref.dtype)
            else:
                o_ref[:, c0:c1] = t.astype(o_ref.dtype)
        col += width
        if o_ref is rkv_ref or o_ref is lora_ref:
            mu_col += width


def _in_proj(x16, w16, cos_t, sin_t, gate_bias, shift_mu, widths, seq, tk):
    m, k = x16.shape
    tm = _tile(seq, IN_PROJ_ROW_TILE)
    tiles_per_seq = seq // tm
    row = lambda n: pl.BlockSpec((tm, n), lambda i: (i, 0))
    vt_rows = A_VAL_DIM + ONES_ROWS
    vt_spec = pl.BlockSpec((1, A_HEADS, tm // tk, vt_rows, tk),
                           lambda i: (i // tiles_per_seq, 0, i % tiles_per_seq, 0, 0))
    out_specs = [row(n) for n in widths]
    out_shape = [jax.ShapeDtypeStruct((m, n), BF16) for n in widths]
    out_specs[1] = vt_spec
    out_shape[1] = jax.ShapeDtypeStruct((m // seq, A_HEADS, seq // tk, vt_rows, tk), BF16)
    halo_blocks = tm // HALO_ROWS
    halo = pl.BlockSpec((HALO_ROWS, k), lambda i: (jnp.maximum(i * halo_blocks - 1, 0), 0))
    const = lambda a: pl.BlockSpec(a.shape, lambda i: (0, 0))
    return pl.pallas_call(
        functools.partial(_in_proj_kernel, tiles_per_seq=tiles_per_seq),
        grid=(m // tm,),
        in_specs=[row(k), halo, pl.BlockSpec(w16.shape, lambda i: (0, 0), pipeline_mode=pl.Buffered(1)),
                  row(V7X_LANES), row(V7X_LANES), const(gate_bias), const(shift_mu)],
        out_specs=out_specs,
        out_shape=out_shape,
        compiler_params=_params("parallel"),
        name="in_proj",
    )(x16, x16, w16, cos_t, sin_t, gate_bias, shift_mu)


def _diff_attn_kernel(lam_ref, g_ref, q_ref, k_ref, v_ref, o_ref, s_buf, *, lam_init):
    i = pl.program_id(2)
    tq = q_ref.shape[1]
    tk = s_buf.shape[2]
    n_heads = s_buf.shape[0]
    head_cols = [slice(h * A_VAL_DIM, (h + 1) * A_VAL_DIM) for h in range(n_heads)]

    def stacked_queries(cols):
        q = (q_ref[0, :, cols].astype(F32) * (A_HEAD_DIM ** -0.5 * LOG2_E)).astype(BF16)
        lane = lax.broadcasted_iota(jnp.int32, q.shape, 1)
        zero = jnp.zeros_like(q)
        return jnp.concatenate([jnp.where(lane < A_HEAD_DIM, q, zero), jnp.where(lane >= A_HEAD_DIM, q, zero)], axis=0)

    qs = [stacked_queries(cols) for cols in head_cols]

    def rows(j):
        return pl.ds(pl.multiple_of(j * tk, tk), tk)

    def next_scores(h, j):
        return _dot_nt(k_ref[0, rows(j), head_cols[h]], qs[h])

    def values_product(h, j, p):
        return _dot(v_ref[0, h, j], p)

    def softmax_step(j, m, s, masked):
        if masked:
            q_off = lax.broadcasted_iota(jnp.int32, (1, 2 * tq), 1) & (tq - 1)
            delta = (i * tq - j * tk) // CHUNK
            s = jnp.concatenate([jnp.where(q_off >= (a - delta) * CHUNK, s[a * CHUNK:(a + 1) * CHUNK], NEG_INF)
                                 for a in range(tk // CHUNK)], axis=0)
        m_new = jnp.maximum(m, jnp.max(s, axis=0, keepdims=True))
        return m_new, jnp.exp2(m - m_new), jnp.exp2(s - m_new).astype(BF16)

    def body(j, carries):
        slot = j & 1
        loaded = [s_buf[h, slot] for h in range(n_heads)]
        s_next = [next_scores(h, j + 1) for h in range(n_heads)]
        stepped = [softmax_step(j, carries[h][0], loaded[h], False) for h in range(n_heads)]
        pvs = [values_product(h, j, p) for h, (_, _, p) in enumerate(stepped)]
        for h in range(n_heads):
            s_buf[h, 1 - slot] = s_next[h]
        return tuple((m_new, alpha * carries[h][1] + pvs[h]) for h, (m_new, alpha, _) in enumerate(stepped))

    for h in range(n_heads):
        s_buf[h, 0] = next_scores(h, 0)
    init = (jnp.full((1, 2 * tq), NEG_INF, F32), jnp.zeros((v_ref.shape[3], 2 * tq), F32))
    n_full = (i * tq) // tk
    carries = lax.fori_loop(0, n_full, body, (init,) * n_heads)

    lam_rows = lam_ref[...]
    lam = (jnp.exp(jnp.sum(lam_rows[0:1] * lam_rows[1:2], -1, keepdims=True))
           - jnp.exp(jnp.sum(lam_rows[2:3] * lam_rows[3:4], -1, keepdims=True)) + lam_init)
    last = n_full & 1
    for h in range(n_heads):
        _, alpha, p = softmax_step(n_full, carries[h][0], s_buf[h, last], True)
        acc = alpha * carries[h][1] + values_product(h, n_full, p)
        out = acc[:A_VAL_DIM] / acc[A_VAL_DIM:A_VAL_DIM + 1]
        o = out[:, :tq] - lam * out[:, tq:]
        o = o * lax.rsqrt(jnp.mean(o * o, axis=0, keepdims=True) + RMS_EPS) * g_ref[...]
        o_ref[0, :, head_cols[h]] = (o * (1.0 - lam_init)).T.astype(o_ref.dtype)


def _diff_attn(qk, vt, lam_rows, subln_g, lam_init):
    b, _, n_ktiles, vt_rows, tk = vt.shape
    s = n_ktiles * tk
    tq = _tile(s, ATTN_Q_TILE)
    nh = ATTN_HEADS_PER_STEP
    width = nh * A_VAL_DIM
    k_blk0 = A_HEADS // nh
    return pl.pallas_call(
        functools.partial(_diff_attn_kernel, lam_init=lam_init),
        grid=(b, A_HEADS // nh, s // tq),
        in_specs=[
            pl.BlockSpec((4, A_HEAD_DIM), lambda bi, h, i: (0, 0)),
            pl.BlockSpec((A_VAL_DIM, 1), lambda bi, h, i: (0, 0)),
            pl.BlockSpec((1, tq, width), lambda bi, h, i: (bi, i, h)),
            pl.BlockSpec((1, s, width), lambda bi, h, i: (bi, 0, k_blk0 + h)),
            pl.BlockSpec((1, nh, n_ktiles, vt_rows, tk), lambda bi, h, i: (bi, h, 0, 0, 0)),
        ],
        out_specs=pl.BlockSpec((1, tq, width), lambda bi, h, i: (bi, i, h)),
        out_shape=jax.ShapeDtypeStruct((b, s, A_HEADS * A_VAL_DIM), BF16),
        scratch_shapes=[pltpu.VMEM((nh, 2, tk, 2 * tq), F32)],
        compiler_params=_params("parallel", "parallel", "arbitrary"),
        name="diff_attn",
    )(lam_rows, subln_g.reshape(A_VAL_DIM, 1), qk, qk, vt)


def _pool_kernel(p_ref, w_ref, scale_ref, o_ref):
    g = pl.program_id(1)
    x = p_ref[0].astype(F32)
    row = lax.broadcasted_iota(jnp.int32, x.shape, 0)
    for gi, window in enumerate(POOL_WINDOWS):
        @pl.when(g == gi)
        def _(window=window):
            ws = x
            shift = 1
            while shift < window:
                ws = ws + jnp.where(row >= shift, pltpu.roll(ws, shift, 0), 0.0)
                shift *= 2
            count = jnp.minimum(row + 1, window).astype(F32)
            d = ws / count - x
            y = _dot(d.astype(BF16), w_ref[0]) * scale_ref[...]
            o_ref[0] = y.astype(o_ref.dtype)


def _pool(u_pool, pool_w16, pool_scale):
    b, s, d = u_pool.shape
    cg = POOL_GROUP_DIM
    return pl.pallas_call(
        _pool_kernel,
        grid=(b, len(POOL_WINDOWS)),
        in_specs=[
            pl.BlockSpec((1, s, cg), lambda bi, g: (bi, 0, g)),
            pl.BlockSpec((1, cg, cg), lambda bi, g: (g, 0, 0)),
            pl.BlockSpec((1, cg), lambda bi, g: (0, g)),
        ],
        out_specs=pl.BlockSpec((1, s, cg), lambda bi, g: (bi, 0, g)),
        out_shape=jax.ShapeDtypeStruct((b, s, d), BF16),
        compiler_params=_params("parallel", "parallel"),
        name="pool_mixer",
    )(u_pool, pool_w16, pool_scale.reshape(1, d))


def _block_mask():
    r = lax.broadcasted_iota(jnp.int32, (RW_QUAD, RW_QUAD), 0)
    c = lax.broadcasted_iota(jnp.int32, (RW_QUAD, RW_QUAD), 1)
    return (r // R_HEAD_DIM) == (c // R_HEAD_DIM)


def _bd(x, mask):
    tiled = jnp.concatenate([x] * (RW_QUAD // R_HEAD_DIM), axis=0)
    return jnp.where(mask, tiled, jnp.zeros_like(tiled))


def _seg_quad(x, ones_bd):
    hi, lo = _split(x)
    return _dot(hi, ones_bd) + _dot(lo, ones_bd)


def _seg_sum(x, ones_bd):
    parts = [_seg_quad(x[:, q * RW_QUAD:(q + 1) * RW_QUAD], ones_bd) for q in range(x.shape[1] // RW_QUAD)]
    return jnp.concatenate(parts, axis=1)


def _chunk_summaries(units, mask):
    c = RW_CHUNK
    ii = lax.broadcasted_iota(jnp.int32, (c, RW_QUAD), 0)
    jj = lax.broadcasted_iota(jnp.int32, (c, RW_QUAD), 1) % c
    strict = ii > jj
    incl = ii >= jj
    eye = (ii == jj).astype(F32)
    bd = lambda x: _bd(x, mask)

    scaled = []
    for r, k, v, a, b, lw, cum in units:
        cum_end = cum[c - 1:c, :]
        at = a * jnp.exp(cum - lw)
        rt = r * jnp.exp(cum)
        e_neg = jnp.exp(-cum)
        w_end = jnp.exp(cum_end)
        bt = b * e_neg
        kt = k * e_neg
        scaled.append((at, rt, bt, kt, bt * w_end, kt * w_end, v, w_end))

    scores = [_mm(jnp.concatenate([at, rt], axis=0), jnp.concatenate([bd(bt), bd(kt)], axis=0), "nt")
              for at, rt, bt, kt, _, _, _, _ in scaled]
    a_ab = [jnp.where(strict, sc[:c, :RW_QUAD], 0.0) for sc in scores]
    a_rb = [jnp.where(incl, sc[c:, :RW_QUAD], 0.0) for sc in scores]
    av = [_mm(jnp.concatenate([jnp.where(strict, sc[:c, RW_QUAD:], 0.0), jnp.where(incl, sc[c:, RW_QUAD:], 0.0)],
                              axis=0), bd(u[6])) for sc, u in zip(scores, scaled)]

    tinv = [a + eye for a in a_ab]
    pk = [_mm(a, bd(a)) for a in a_ab]
    for _ in range(int(math.log2(c)) - 2):
        both = [_mm(jnp.concatenate([p, t], axis=0), bd(p)) for p, t in zip(pk, tinv)]
        tinv = [t + b[c:] for t, b in zip(tinv, both)]
        pk = [b[:c] for b in both]
    tinv = [t + _mm(t, bd(p)) for t, p in zip(tinv, pk)]

    side = lambda x, y: jnp.concatenate([x, y], axis=1)
    out = []
    for t, arb, avu, (at, rt, _, _, bh, kh, v, w_end) in zip(tinv, a_rb, av, scaled):
        x = _mm(t, side(bd(at), bd(avu[:c])))
        xa, xu = x[:, :RW_QUAD], x[:, RW_QUAD:]
        ry = _mm(arb, side(bd(xa), bd(xu)))
        z = _mm(bh, x, "tn")
        z1 = jnp.where(mask, z[:, :RW_QUAD], 0.0)
        z2 = jnp.where(mask, z[:, RW_QUAD:] + _mm(kh, v, "tn"), 0.0)
        fold = lambda z: functools.reduce(jnp.add, [z[h * c:(h + 1) * c] for h in range(RW_QUAD // R_HEAD_DIM)])
        out.append((rt + ry[:, :RW_QUAD], avu[c:] + ry[:, RW_QUAD:], fold(z1) + eye * w_end, fold(z2)))
    return out


def _rwkv_kernel(rkv_ref, lora_ref, vec_ref, w2_ref, a2_ref, g2_ref, o_ref, state_ref):
    ts = rkv_ref.shape[1]
    d = D_MODEL

    @pl.when(pl.program_id(1) == 0)
    def _():
        state_ref[...] = jnp.zeros_like(state_ref)

    xs = rkv_ref[0].astype(F32)
    ls = lora_ref[0].astype(F32)
    r = xs[:, :d]
    k = xs[:, d:2 * d]
    v = xs[:, 2 * d:]
    vec = vec_ref[...]
    w0, a0, k_k, k_a, r_k = vec[0:1], vec[1:2], vec[2:3], vec[3:4], vec[4:5]

    xa0 = LORA_XW_PAD
    xg0 = LORA_XW_PAD + LORA_XA_PAD
    z = w0 + _dot(jnp.tanh(ls[:, :xa0]).astype(BF16), w2_ref[...])
    lw = -math.exp(-0.5) * _sigmoid(z)
    a_sig = _sigmoid(a0 + _dot(ls[:, xa0:xg0].astype(BF16), a2_ref[...]))
    gate = _dot(_sigmoid(ls[:, xg0:]).astype(BF16), g2_ref[...])

    mask = _block_mask()
    ones_bd = mask.astype(BF16)
    kk = k * k_k
    kk = kk * lax.rsqrt(jnp.maximum(_seg_sum(kk * kk, ones_bd), KK_NORM_EPS * KK_NORM_EPS))
    k2 = k * (1.0 + (a_sig - 1.0) * k_a)
    a_vec = -kk
    b_vec = kk * a_sig
    bonus = _seg_sum(r * k2 * r_k, ones_bd) * v

    ti = lax.broadcasted_iota(jnp.int32, (ts, ts), 0)
    tj = lax.broadcasted_iota(jnp.int32, (ts, ts), 1)
    tri = ((tj <= ti) & (tj // RW_CHUNK == ti // RW_CHUNK)).astype(BF16)
    lw_hi, lw_lo = _split(lw)
    cum = _dot(tri, lw_hi) + _dot(tri, lw_lo)

    windows = [(slice(c * RW_CHUNK, (c + 1) * RW_CHUNK), slice(q * RW_QUAD, (q + 1) * RW_QUAD))
               for c in range(ts // RW_CHUNK) for q in range(d // RW_QUAD)]
    units = [tuple(x[rows, cols] for x in (r, k2, v, a_vec, b_vec, lw, cum)) for rows, cols in windows]
    summaries = _chunk_summaries(units, mask)

    inv_n = 1.0 / R_HEAD_DIM
    lnx_g, lnx_b = vec[5:6], vec[6:7]
    n_quads = d // RW_QUAD
    states = [state_ref[:, q * RW_QUAD:(q + 1) * RW_QUAD] for q in range(n_quads)]
    start_states = []
    for (rows, cols), (rp, y0, m, n) in zip(windows, summaries):
        q = cols.start // RW_QUAD
        start_bd = _bd(states[q], mask)
        start_states.append(start_bd)
        states[q] = _mm(m, start_bd) + n
    for q in range(n_quads):
        state_ref[:, q * RW_QUAD:(q + 1) * RW_QUAD] = states[q]

    ys = [_mm(rp, h0) + y0 for (rp, y0, _, _), h0 in zip(summaries, start_states)]
    mus = [_seg_quad(y, ones_bd) * inv_n for y in ys]
    dys = [y - mu for y, mu in zip(ys, mus)]
    variances = [_seg_quad(dy * dy, ones_bd) * inv_n for dy in dys]
    for (rows, cols), dy, var in zip(windows, dys, variances):
        yn = dy * lax.rsqrt(var + RWKV_GN_EPS) * lnx_g[:, cols] + lnx_b[:, cols]
        o_ref[0, rows, cols] = ((yn + bonus[rows, cols]) * gate[rows, cols]).astype(o_ref.dtype)


def _rwkv(u_rkv, u_lora, vecs, w2p, a2p, g2p):
    b, s, w3 = u_rkv.shape
    d = D_MODEL
    ts = _tile(s, RW_TILE)

    def tile_spec(width):
        return pl.BlockSpec((1, ts, width), lambda bi, t: (bi, t, 0))

    def const_spec(shape):
        return pl.BlockSpec(shape, lambda bi, t: (0,) * len(shape))

    return pl.pallas_call(
        _rwkv_kernel,
        grid=(b, s // ts),
        in_specs=[tile_spec(w3), tile_spec(LORA_PAD), const_spec(vecs.shape),
                  const_spec(w2p.shape), const_spec(a2p.shape), const_spec(g2p.shape)],
        out_specs=tile_spec(d),
        out_shape=jax.ShapeDtypeStruct((b, s, d), BF16),
        scratch_shapes=[pltpu.VMEM((R_HEAD_DIM, d), F32)],
        compiler_params=_params("parallel", "arbitrary"),
        name="rwkv7",
    )(u_rkv, u_lora, vecs, w2p, a2p, g2p)


def _merge_kernel(ya_ref, yb_ref, yc_ref, gates_ref, h_ref, wa_ref, wb_ref, wc_ref, wo_ref,
                  g_ref, b_ref, h32_ref, h16_ref):
    d = D_MODEL
    merged = jnp.zeros((ya_ref.shape[0], d), F32)
    for idx, (y_ref, w_ref) in enumerate(((ya_ref, wa_ref), (yb_ref, wb_ref), (yc_ref, wc_ref))):
        gate = gates_ref[:, idx * d:(idx + 1) * d].astype(F32)
        merged = merged + gate * _dot(y_ref[...], w_ref[...])
    z = DN_ALPHA * h_ref[...] + _dot(merged.astype(BF16), wo_ref[...])
    h = _ln(z, g_ref[...], b_ref[...], LN_EPS)
    h32_ref[...] = h
    h16_ref[...] = h.astype(BF16)


def _merge(ya, yb, yc, gates, h32, wa, wb, wc, wo, ln_g, ln_b):
    t, d = h32.shape
    tm = _tile(t, FUSED_ROW_TILE)
    row = pl.BlockSpec((tm, d), lambda i: (i, 0))
    wspec = pl.BlockSpec((d, d), lambda i: (0, 0))
    vec = pl.BlockSpec((1, d), lambda i: (0, 0))
    return pl.pallas_call(
        _merge_kernel,
        grid=(t // tm,),
        in_specs=[row, row, row, pl.BlockSpec((tm, 3 * d), lambda i: (i, 0)),
                  row, wspec, wspec, wspec, wspec, vec, vec],
        out_specs=[row, row],
        out_shape=[jax.ShapeDtypeStruct((t, d), F32), jax.ShapeDtypeStruct((t, d), BF16)],
        compiler_params=_params("parallel"),
        name="merge_out",
    )(ya, yb, yc, gates, h32, wa, wb, wc, wo, ln_g.reshape(1, d), ln_b.reshape(1, d))


def _cross_kernel(h16_ref, h32_ref, kv_ref, wq_ref, wo_ref, g_ref, b_ref, o32_ref, o16_ref):
    d = D_MODEL
    q = _dot(h16_ref[0], wq_ref[...]).astype(BF16)
    kv = kv_ref[0]
    head_cols = [slice(hd * X_HEAD_DIM, (hd + 1) * X_HEAD_DIM) for hd in range(X_HEADS)]
    scores = [_dot_nt(q[:, cols], kv[:, cols]) * (X_HEAD_DIM ** -0.5) for cols in head_cols]
    probs = [jnp.exp(s - jnp.max(s, -1, keepdims=True)) for s in scores]
    outs = [_dot(p.astype(BF16), kv[:, d + cols.start:d + cols.stop]) for p, cols in zip(probs, head_cols)]
    attn = jnp.concatenate([(o / jnp.sum(p, -1, keepdims=True)).astype(BF16) for o, p in zip(outs, probs)], axis=1)
    z = DN_ALPHA * h32_ref[0] + _dot(attn, wo_ref[...])
    h = _ln(z, g_ref[...], b_ref[...], LN_EPS)
    o32_ref[0] = h
    o16_ref[0] = h.astype(BF16)


def _cross(h16, h32, kv, wq, wo, ln_g, ln_b):
    b, s, d = h32.shape
    tm = _tile(s, FUSED_ROW_TILE)
    row = pl.BlockSpec((1, tm, d), lambda bi, i: (bi, i, 0))
    wspec = pl.BlockSpec((d, d), lambda bi, i: (0, 0))
    vec = pl.BlockSpec((1, d), lambda bi, i: (0, 0))
    n_mem = kv.shape[1]
    return pl.pallas_call(
        _cross_kernel,
        grid=(b, s // tm),
        in_specs=[row, row, pl.BlockSpec((1, n_mem, 2 * d), lambda bi, i: (bi, 0, 0)), wspec, wspec, vec, vec],
        out_specs=[row, row],
        out_shape=[jax.ShapeDtypeStruct((b, s, d), F32), jax.ShapeDtypeStruct((b, s, d), BF16)],
        compiler_params=_params("parallel", "parallel"),
        name="cross_attn",
    )(h16, h32, kv, wq, wo, ln_g.reshape(1, d), ln_b.reshape(1, d))


def _mlp_kernel(h16_ref, h32_ref, w1_ref, w2_ref, g_ref, b_ref, o32_ref, o16_ref):
    x = h16_ref[...]
    dff = w1_ref.shape[1]
    tf = min(MLP_FF_TILE, dff)
    acc = DN_ALPHA * h32_ref[...]
    for f in range(dff // tf):
        a = jnp.maximum(_dot(x, w1_ref[:, f * tf:(f + 1) * tf]), 0.0)
        acc = acc + _dot((a * a).astype(BF16), w2_ref[f * tf:(f + 1) * tf, :])
    h = _ln(acc, g_ref[...], b_ref[...], LN_EPS)
    o32_ref[...] = h
    o16_ref[...] = h.astype(BF16)


def _mlp(h16, h32, w1, w2, ln_g, ln_b):
    t, d = h32.shape
    dff = w1.shape[1]
    tm = _tile(t, MLP_ROW_TILE)
    row = pl.BlockSpec((tm, d), lambda i: (i, 0))
    vec = pl.BlockSpec((1, d), lambda i: (0, 0))
    resident = pl.Buffered(1)
    return pl.pallas_call(
        _mlp_kernel,
        grid=(t // tm,),
        in_specs=[row, row, pl.BlockSpec((d, dff), lambda i: (0, 0), pipeline_mode=resident),
                  pl.BlockSpec((dff, d), lambda i: (0, 0), pipeline_mode=resident), vec, vec],
        out_specs=[row, row],
        out_shape=[jax.ShapeDtypeStruct((t, d), F32), jax.ShapeDtypeStruct((t, d), BF16)],
        compiler_params=_params("parallel"),
        name="mlp",
    )(h16, h32, w1, w2, ln_g.reshape(1, d), ln_b.reshape(1, d))


def _pad_rows(w, rows):
    return jnp.pad(w, ((0, rows - w.shape[0]), (0, 0)))


def _lora_layout(w):
    pad = lambda t, n: jnp.pad(t, [(0, 0)] * (t.ndim - 1) + [(0, n - t.shape[-1])])
    xw = w[..., :DECAY_LORA]
    xa = w[..., DECAY_LORA:DECAY_LORA + AAA_LORA]
    xg = w[..., DECAY_LORA + AAA_LORA:]
    return jnp.concatenate([pad(xw, LORA_XW_PAD), pad(xa, LORA_XA_PAD), pad(xg, LORA_XG_PAD)], axis=-1)


def kernel(x, mem, positions, ln_in_g, ln_in_b, w_in, b_gate, lam_q1, lam_k1, lam_q2, lam_k2, attn_subln_g, w_br_attn, pool_w, pool_scale, w_br_pool, rwkv_mu, rwkv_w0, rwkv_w2, rwkv_a0, rwkv_a2, rwkv_g2, rwkv_k_k, rwkv_k_a, rwkv_r_k, rwkv_lnx_g, rwkv_lnx_b, w_br_rwkv, w_out, ln1_g, ln1_b, w_xq, w_xkv, w_xo, ln2_g, ln2_b, w_ff1, w_ff2, ln3_g, ln3_b):
    bsz, seq, d = x.shape
    t = bsz * seq
    n_mem = mem.shape[1]
    depth = w_in.shape[0]

    inv_freq = 1.0 / (ROPE_THETA ** (jnp.arange(0, A_HEAD_DIM, 2, dtype=F32) / A_HEAD_DIM))
    ang = positions.astype(F32).reshape(t, 1) * inv_freq
    cos, sin = jnp.cos(ang), jnp.sin(ang)
    cos_t = jnp.concatenate([cos, cos, cos, cos], axis=-1)
    sin_t = jnp.concatenate([-sin, sin, -sin, sin], axis=-1)

    c_qk = 2 * A_HEADS * 2 * A_HEAD_DIM
    c_v = c_qk + A_HEADS * A_VAL_DIM
    c_pool = c_v + d
    c_rkv = c_pool + 3 * d
    c_lora = c_rkv + DECAY_LORA + AAA_LORA + GATE_LORA

    h32, h16 = _ln_in(x.reshape(t, d), ln_in_g, ln_in_b)
    mem2d = mem.reshape(bsz * n_mem, d)

    for l in range(depth):
        w = w_in[l]
        w16 = jnp.concatenate([w[:, :c_rkv], _lora_layout(w[:, c_rkv:c_lora]), w[:, c_lora:]], axis=1).astype(BF16)
        widths = (c_qk, c_v - c_qk, c_pool - c_v, c_rkv - c_pool, LORA_PAD, w.shape[1] - c_lora)
        mu = rwkv_mu[l]
        shift_mu = jnp.concatenate([mu[:3 * d], _lora_layout(mu[3 * d:])]).reshape(1, 3 * d + LORA_PAD)
        qk, vt, u_pool, u_rkv, u_lora, gates = _in_proj(h16, w16, cos_t, sin_t, b_gate[l].reshape(1, 3 * d), shift_mu,
                                                        widths, seq, min(ATTN_K_TILE, seq))
        qk = qk.reshape(bsz, seq, c_qk)
        u_pool = u_pool.reshape(bsz, seq, d)
        u_rkv = u_rkv.reshape(bsz, seq, 3 * d)
        u_lora = u_lora.reshape(bsz, seq, LORA_PAD)

        lam_init = 0.8 - 0.6 * math.exp(-0.3 * l)
        lam_rows = jnp.stack([lam_q1[l], lam_k1[l], lam_q2[l], lam_k2[l]]).astype(F32)
        y_a = _diff_attn(qk, vt, lam_rows, attn_subln_g[l], lam_init)

        y_b = _pool(u_pool, pool_w[l].astype(BF16), pool_scale[l])

        vecs = jnp.stack([rwkv_w0[l], rwkv_a0[l], rwkv_k_k[l], rwkv_k_a[l], rwkv_r_k[l].reshape(d),
                          rwkv_lnx_g[l], rwkv_lnx_b[l], jnp.zeros((d,), F32)])
        w2p = _pad_rows(rwkv_w2[l], LORA_XW_PAD).astype(BF16)
        a2p = _pad_rows(rwkv_a2[l], LORA_XA_PAD).astype(BF16)
        g2p = _pad_rows(rwkv_g2[l], LORA_XG_PAD).astype(BF16)
        y_c = _rwkv(u_rkv, u_lora, vecs, w2p, a2p, g2p)

        h32, h16 = _merge(y_a.reshape(t, d), y_b.reshape(t, d), y_c.reshape(t, d), gates, h32,
                          w_br_attn[l].astype(BF16), w_br_pool[l].astype(BF16), w_br_rwkv[l].astype(BF16),
                          w_out[l].astype(BF16), ln1_g[l], ln1_b[l])

        kv = _proj(mem2d, w_xkv[l].astype(BF16), BF16, name="proj_kv").reshape(bsz, n_mem, 2 * d)
        h32, h16 = _cross(h16.reshape(bsz, seq, d), h32.reshape(bsz, seq, d), kv, w_xq[l].astype(BF16),
                          w_xo[l].astype(BF16), ln2_g[l], ln2_b[l])
        h32, h16 = h32.reshape(t, d), h16.reshape(t, d)

        h32, h16 = _mlp(h16, h32, w_ff1[l].astype(BF16), w_ff2[l].astype(BF16), ln3_g[l], ln3_b[l])

    return h32.reshape(bsz, seq, d)
```
